```python
import math
import jax
import jax.numpy as jnp
from jax import lax
import numpy as np

D_MODEL = 1024
BATCH = 32
SEQ = 2048
DEPTH = 4

HEAD_DIM = 64
A_Q_HEADS = 8
A_KV_HEADS = 2
B_HEADS = 8
B_PATTERNS = ((128, 1), (512, 4), (2048, 16))
GRID_W = 64
AXIAL_THETA = 10000.0
ROPE_THETA = 500000.0
ROPE_DIMS = HEAD_DIM // 4
Q_BLOCK = 128
D_RNN = D_MODEL
LRU_BLOCKS = 8
LRU_BW = D_RNN // LRU_BLOCKS
CONV_WIDTH = 4
CONV_PAD = (1, 2)
LRU_C = 8.0
D_FF = -(-(8 * D_MODEL) // (3 * 256)) * 256
N_EVEN = (DEPTH + 1) // 2
N_ODD = DEPTH // 2
A_Q_W = A_Q_HEADS * HEAD_DIM
A_KV_W = A_KV_HEADS * HEAD_DIM
B_W = B_HEADS * HEAD_DIM
EVEN_IN = A_Q_W + 2 * A_KV_W + len(B_PATTERNS) * 3 * B_W
EVEN_CAT = A_Q_W + B_W
EPS = 1e-6
NEG_INF = -1e30

kernel_name = "hybrid_axial_gqa_dilated_rglru_encoder"


def _rmsnorm(x, g):
    xf = x.astype(jnp.float32)
    y = xf * lax.rsqrt(jnp.mean(xf * xf, axis=-1, keepdims=True) + EPS)
    return (y * g.astype(jnp.float32)).astype(x.dtype)


def _rope(x, pos, theta):
    dim = x.shape[-1]
    half = dim // 2
    inv = jnp.power(jnp.float32(theta), -jnp.arange(half, dtype=jnp.float32) * (2.0 / dim))
    ang = pos.astype(jnp.float32)[:, None] * inv[None, :]
    cos = jnp.cos(ang)[:, None, :]
    sin = jnp.sin(ang)[:, None, :]
    xf = x.astype(jnp.float32)
    x1, x2 = xf[..., :half], xf[..., half:]
    return jnp.concatenate([x1 * cos - x2 * sin, x2 * cos + x1 * sin], axis=-1).astype(x.dtype)


def _axial_rope(x, row, col):
    h = x.shape[-1] // 2
    return jnp.concatenate([_rope(x[..., :h], row, AXIAL_THETA), _rope(x[..., h:], col, AXIAL_THETA)], axis=-1)


def _partial_rope(x, pos):
    return jnp.concatenate([_rope(x[..., :ROPE_DIMS], pos, ROPE_THETA), x[..., ROPE_DIMS:]], axis=-1)


def _gqa_block_sweep(q, k, v):
    B, S, Hq, dh = q.shape
    Hkv = k.shape[2]
    rep = Hq // Hkv
    nb = S // Q_BLOCK
    qb = q.reshape(B, nb, Q_BLOCK, Hkv, rep, dh).transpose(1, 0, 2, 3, 4, 5)

    def one_block(q_blk):
        s = jnp.einsum('bqgrd,bkgd->bgrqk', q_blk, k).astype(jnp.float32)
        p = jax.nn.softmax(s, axis=-1).astype(v.dtype)
        return jnp.einsum('bgrqk,bkgd->bqgrd', p, v)

    o = lax.map(one_block, qb)
    return o.transpose(1, 0, 2, 3, 4, 5).reshape(B, S, Hq, dh)


def _banded_attention(q, k, v, radius):
    *lead, M, dh = q.shape
    blk = radius
    nb = -(-M // blk)
    Mp = nb * blk
    lead_pad = [(0, 0)] * len(lead)
    qb = jnp.pad(q, lead_pad + [(0, Mp - M), (0, 0)]).reshape(*lead, nb, blk, dh)

    def windows(t):
        tp = jnp.pad(t, lead_pad + [(blk, Mp - M + blk), (0, 0)])
        return jnp.concatenate(
            [tp[..., i * blk:(i + nb) * blk, :].reshape(*lead, nb, blk, dh) for i in range(3)], axis=-2)

    kw = windows(k)
    vw = windows(v)
    s = jnp.einsum('...nqd,...nkd->...nqk', qb, kw).astype(jnp.float32)
    qpos = jnp.arange(nb)[:, None] * blk + jnp.arange(blk)[None, :]
    kpos = jnp.arange(nb)[:, None] * blk + jnp.arange(3 * blk)[None, :] - blk
    valid = ((jnp.abs(qpos[:, :, None] - kpos[:, None, :]) <= radius)
             & (kpos >= 0)[:, None, :] & (kpos < M)[:, None, :])
    s = jnp.where(valid, s, NEG_INF)
    lse = jax.nn.logsumexp(s, axis=-1)
    p = jnp.exp(s - lse[..., None]).astype(v.dtype)
    o = jnp.einsum('...nqk,...nkd->...nqd', p, vw).reshape(*lead, Mp, dh)[..., :M, :]
    return o, lse.reshape(*lead, Mp)[..., :M]


def _dilated_group(q, k, v, window, dilation):
    B, S, H, dh = q.shape
    M = S // dilation
    radius = window // (2 * dilation)

    def to_sub(t):
        return t.reshape(B, M, dilation, H, dh).transpose(0, 2, 3, 1, 4)

    o, lse = _banded_attention(to_sub(q), to_sub(k), to_sub(v), radius)
    o = o.transpose(0, 3, 1, 2, 4).reshape(B, S, H, dh)
    lse = lse.transpose(0, 3, 1, 2).reshape(B, S, H)
    return o, lse


def _even_mixer(h, w_in, q_norm, k_norm, w_out, row, col, pos):
    B, S, _ = h.shape
    scale = HEAD_DIM ** -0.5
    proj = h @ w_in
    o0 = 0
    a_q = proj[..., o0:o0 + A_Q_W].reshape(B, S, A_Q_HEADS, HEAD_DIM); o0 += A_Q_W
    a_k = proj[..., o0:o0 + A_KV_W].reshape(B, S, A_KV_HEADS, HEAD_DIM); o0 += A_KV_W
    a_v = proj[..., o0:o0 + A_KV_W].reshape(B, S, A_KV_HEADS, HEAD_DIM); o0 += A_KV_W
    a_q = _axial_rope(_rmsnorm(a_q, q_norm), row, col)
    a_k = _axial_rope(_rmsnorm(a_k, k_norm), row, col)
    a_o = _gqa_block_sweep(a_q * scale, a_k, a_v).reshape(B, S, A_Q_W)
    outs = []
    lses = []
    for window, dilation in B_PATTERNS:
        b_q = proj[..., o0:o0 + B_W].reshape(B, S, B_HEADS, HEAD_DIM); o0 += B_W
        b_k = proj[..., o0:o0 + B_W].reshape(B, S, B_HEADS, HEAD_DIM); o0 += B_W
        b_v = proj[..., o0:o0 + B_W].reshape(B, S, B_HEADS, HEAD_DIM); o0 += B_W
        o, lse = _dilated_group(_partial_rope(b_q, pos) * scale, _partial_rope(b_k, pos), b_v, window, dilation)
        outs.append(o)
        lses.append(lse)
    wts = jax.nn.softmax(jnp.stack(lses, axis=0), axis=0)
    b_o = jnp.einsum('gbsh,gbshd->bshd', wts, jnp.stack(outs, axis=0).astype(jnp.float32))
    b_o = b_o.astype(h.dtype).reshape(B, S, B_W)
    return jnp.concatenate([a_o, b_o], axis=-1) @ w_out


def _rglru(x, w_a, b_a, w_x, b_x, lam, reverse):
    B, S, _ = x.shape
    xb = x.reshape(B, S, LRU_BLOCKS, LRU_BW)
    r = jax.nn.sigmoid(jnp.einsum('bsnc,ncd->bsnd', xb, w_a.astype(jnp.float32)).reshape(B, S, D_RNN)
                       + b_a.astype(jnp.float32))
    i = jax.nn.sigmoid(jnp.einsum('bsnc,ncd->bsnd', xb, w_x.astype(jnp.float32)).reshape(B, S, D_RNN)
                       + b_x.astype(jnp.float32))
    log_a = -LRU_C * r * jax.nn.softplus(-lam.astype(jnp.float32))
    a = jnp.exp(log_a)
    b = jnp.sqrt(-jnp.expm1(2.0 * log_a)) * (i * x)

    def combine(left, right):
        a1, b1 = left
        a2, b2 = right
        return a1 * a2, a2 * b1 + b2

    _, hs = lax.associative_scan(combine, (a, b), axis=1, reverse=reverse)
    return hs


def _odd_mixer(h, w_in, conv_w, conv_b, ga_w, ga_b, gx_w, gx_b, lam, w_out):
    proj = h @ w_in
    gate = jax.nn.gelu(proj[..., :D_RNN], approximate=True)
    u = proj[..., D_RNN:]
    u = lax.conv_general_dilated(u, conv_w[:, None, :], window_strides=(1,), padding=[CONV_PAD],
                                 dimension_numbers=('NWC', 'WIO', 'NWC'),
                                 feature_group_count=D_RNN) + conv_b
    uf = u.astype(jnp.float32)
    hs = (_rglru(uf, ga_w[0], ga_b[0], gx_w[0], gx_b[0], lam[0], reverse=False)
          + _rglru(uf, ga_w[1], ga_b[1], gx_w[1], gx_b[1], lam[1], reverse=True))
    return (hs.astype(h.dtype) * gate) @ w_out


def _swiglu(h, w_gate_up, w_down):
    gu = h @ w_gate_up
    return (jax.nn.silu(gu[..., :D_FF]) * gu[..., D_FF:]) @ w_down


def setup_inputs(seed: int = 0) -> dict:
    key = jax.random.key(seed)
    ks = jax.random.split(key, 20)
    f32 = jnp.float32

    def nrm(k, shape, fan_in):
        return jax.random.normal(k, shape, f32) * (fan_in ** -0.5)

    def gain(k, shape):
        return 1.0 + 0.02 * jax.random.normal(k, shape, f32)

    a0 = jax.random.uniform(ks[11], (N_ODD, 2, D_RNN), f32, minval=0.9, maxval=0.999) ** (1.0 / LRU_C)
    return {
        "x": jax.random.normal(ks[0], (BATCH, SEQ, D_MODEL), f32),
        "mix_norm": gain(ks[1], (DEPTH, D_MODEL)),
        "ffn_norm": gain(ks[2], (DEPTH, D_MODEL)),
        "final_norm": gain(ks[3], (D_MODEL,)),
        "even_w_in": nrm(ks[4], (N_EVEN, D_MODEL, EVEN_IN), D_MODEL),
        "even_q_norm": gain(ks[5], (N_EVEN, HEAD_DIM)),
        "even_k_norm": gain(ks[6], (N_EVEN, HEAD_DIM)),
        "even_w_out": nrm(ks[7], (N_EVEN, EVEN_CAT, D_MODEL), EVEN_CAT),
        "odd_w_in": nrm(ks[8], (N_ODD, D_MODEL, 2 * D_RNN), D_MODEL),
        "odd_conv_w": nrm(ks[9], (N_ODD, CONV_WIDTH, D_RNN), CONV_WIDTH),
        "odd_conv_b": 0.02 * jax.random.normal(ks[10], (N_ODD, D_RNN), f32),
        "odd_gate_a_w": nrm(ks[12], (N_ODD, 2, LRU_BLOCKS, LRU_BW, LRU_BW), LRU_BW),
        "odd_gate_a_b": 0.02 * jax.random.normal(ks[13], (N_ODD, 2, D_RNN), f32),
        "odd_gate_x_w": nrm(ks[14], (N_ODD, 2, LRU_BLOCKS, LRU_BW, LRU_BW), LRU_BW),
        "odd_gate_x_b": 0.02 * jax.random.normal(ks[15], (N_ODD, 2, D_RNN), f32),
        "odd_lambda": jnp.log(a0) - jnp.log1p(-a0),
        "odd_w_out": nrm(ks[16], (N_ODD, D_RNN, D_MODEL), D_RNN),
        "ffn_w_gate_up": nrm(ks[17], (DEPTH, D_MODEL, 2 * D_FF), D_MODEL),
        "ffn_w_down": nrm(ks[18], (DEPTH, D_FF, D_MODEL), D_FF),
    }


def reference(x, mix_norm, ffn_norm, final_norm, even_w_in, even_q_norm, even_k_norm, even_w_out,
              odd_w_in, odd_conv_w, odd_conv_b, odd_gate_a_w, odd_gate_a_b, odd_gate_x_w, odd_gate_x_b,
              odd_lambda, odd_w_out, ffn_w_gate_up, ffn_w_down):
    B, S, _ = x.shape
    rows = S // GRID_W
    row = jnp.broadcast_to(jnp.arange(rows, dtype=jnp.int32)[:, None], (rows, GRID_W)).reshape(-1)
    col = jnp.broadcast_to(jnp.arange(GRID_W, dtype=jnp.int32)[None, :], (rows, GRID_W)).reshape(-1)
    pos = jnp.arange(S, dtype=jnp.int32)
    for layer in range(DEPTH):
        j = layer // 2
        h = _rmsnorm(x, mix_norm[layer])
        if layer % 2 == 0:
            x = x + _even_mixer(h, even_w_in[j], even_q_norm[j], even_k_norm[j], even_w_out[j], row, col, pos)
        else:
            x = x + _odd_mixer(h, odd_w_in[j], odd_conv_w[j], odd_conv_b[j], odd_gate_a_w[j], odd_gate_a_b[j],
                               odd_gate_x_w[j], odd_gate_x_b[j], odd_lambda[j], odd_w_out[j])
        x = x + _swiglu(_rmsnorm(x, ffn_norm[layer]), ffn_w_gate_up[layer], ffn_w_down[layer])
    return _rmsnorm(x, final_norm)
```

```python
import functools

import jax
import jax.numpy as jnp
from jax import lax
from jax.experimental import pallas as pl
from jax.experimental.pallas import tpu as pltpu

D_MODEL = 1024
HEAD_DIM = 64
A_Q_HEADS = 8
A_KV_HEADS = 2
B_HEADS = 8
B_DILATIONS = (1, 4, 16)
B_RADIUS = 64
GRID_W = 64
AXIAL_THETA = 10000.0
ROPE_THETA = 500000.0
ROPE_DIMS = HEAD_DIM // 4
D_RNN = D_MODEL
LRU_BLOCKS = 8
LRU_BW = D_RNN // LRU_BLOCKS
LRU_C = 8.0
D_FF = 2816
EPS = 1e-6
NEG_INF = -1e30

LANES = 128
A_Q_W = A_Q_HEADS * HEAD_DIM
B_W = B_HEADS * HEAD_DIM
EVEN_COLS = 1024 + 3 * 3 * B_W
A_HEAD_ORDER = (0, 4, 1, 5, 2, 6, 3, 7)

VMEM_LIMIT = 56 * 1024 * 1024

_BF = jnp.bfloat16
_F32 = jnp.float32


def _cparams(*sem):
    return pltpu.CompilerParams(dimension_semantics=sem, vmem_limit_bytes=VMEM_LIMIT)


def _rms_rows(x, g):
    ms = jnp.mean(x * x, axis=-1, keepdims=True)
    return x * lax.rsqrt(ms + EPS) * g


def _sigmoid(x):
    return 0.5 * (1.0 + jnp.tanh(0.5 * x))


def _lane(shape):
    return lax.broadcasted_iota(jnp.int32, shape, len(shape) - 1)


def _norm_proj_kernel(x_ref, g_ref, w_ref, o_ref, h_ref):
    @pl.when(pl.program_id(1) == 0)
    def _():
        h_ref[...] = _rms_rows(x_ref[...], g_ref[...]).astype(_BF)

    o_ref[...] = jnp.dot(h_ref[...], w_ref[...], preferred_element_type=_F32).astype(o_ref.dtype)


def _norm_proj(x, g, w, *, tm, tn):
    n, d = x.shape
    cols = w.shape[1]
    return pl.pallas_call(
        _norm_proj_kernel,
        grid=(n // tm, cols // tn),
        in_specs=[
            pl.BlockSpec((tm, d), lambda i, j: (i, 0)),
            pl.BlockSpec((1, d), lambda i, j: (0, 0)),
            pl.BlockSpec((d, tn), lambda i, j: (0, j)),
        ],
        out_specs=pl.BlockSpec((tm, tn), lambda i, j: (i, j)),
        out_shape=jax.ShapeDtypeStruct((n, cols), _BF),
        scratch_shapes=[pltpu.VMEM((tm, d), _BF)],
        compiler_params=_cparams("parallel", "arbitrary"),
        name="norm_proj",
    )(x, g, w)


def _gelu_tanh(x):
    return 0.5 * x * (1.0 + jnp.tanh(0.7978845608028654 * (x + 0.044715 * (x * x * x))))


def _odd_in_kernel(x_ref, g_ref, w_ref, gate_ref, u_ref, h_ref):
    j = pl.program_id(1)

    @pl.when(j == 0)
    def _():
        h_ref[...] = _rms_rows(x_ref[...], g_ref[...]).astype(_BF)
        acc = jnp.dot(h_ref[...], w_ref[...], preferred_element_type=_F32)
        gate_ref[...] = _gelu_tanh(acc).astype(gate_ref.dtype)

    @pl.when(j == 1)
    def _():
        u_ref[...] = jnp.dot(h_ref[...], w_ref[...], preferred_element_type=_F32)


def _odd_in(x, g, w, *, tm):
    n, d = x.shape
    return pl.pallas_call(
        _odd_in_kernel,
        grid=(n // tm, 2),
        in_specs=[
            pl.BlockSpec((tm, d), lambda i, j: (i, 0)),
            pl.BlockSpec((1, d), lambda i, j: (0, 0)),
            pl.BlockSpec((d, D_RNN), lambda i, j: (0, j)),
        ],
        out_specs=[
            pl.BlockSpec((tm, D_RNN), lambda i, j: (i, 0)),
            pl.BlockSpec((tm, D_RNN), lambda i, j: (i, 0)),
        ],
        out_shape=[
            jax.ShapeDtypeStruct((n, D_RNN), _BF),
            jax.ShapeDtypeStruct((n, D_RNN), _F32),
        ],
        scratch_shapes=[pltpu.VMEM((tm, d), _BF)],
        compiler_params=_cparams("parallel", "arbitrary"),
        name="odd_in",
    )(x, g, w)


def _rope_slab(x, cos, sin_signed, half):
    lane = _lane(x.shape)
    fwd = pltpu.roll(x, LANES - half, 1)
    bwd = pltpu.roll(x, half, 1)
    partner = jnp.where((lane % (2 * half)) < half, fwd, bwd)
    return x * cos + partner * sin_signed


def _head_rms_slab(x, g):
    lo = _lane(x.shape) < HEAD_DIM
    sq = x * x
    ms_lo = jnp.sum(jnp.where(lo, sq, 0.0), axis=-1, keepdims=True) * (1.0 / HEAD_DIM)
    ms_hi = jnp.sum(jnp.where(lo, 0.0, sq), axis=-1, keepdims=True) * (1.0 / HEAD_DIM)
    rs = jnp.where(lo, lax.rsqrt(ms_lo + EPS), lax.rsqrt(ms_hi + EPS))
    return x * rs * g


def _attn_a_kernel(q_ref, k_ref, v_ref, cq_ref, sq_ref, ck_ref, sk_ref, gq_ref, gk_ref, o_ref, kt_ref, *, tq):
    @pl.when(pl.program_id(1) == 0)
    def _():
        k = _head_rms_slab(k_ref[...].astype(_F32), gk_ref[...])
        k = _rope_slab(k, ck_ref[...], sk_ref[...], 16)
        kt_ref[...] = k.T.astype(_BF)

    lane = _lane((tq, LANES))
    lo = lane < HEAD_DIM
    cq = cq_ref[...]
    sq = sq_ref[...]
    gq = gq_ref[...]
    q_lo, q_hi = [], []
    for j in range(A_Q_W // LANES):
        q = q_ref[:, j * LANES:(j + 1) * LANES].astype(_F32)
        q = _rope_slab(_head_rms_slab(q, gq), cq, sq, 16) * (HEAD_DIM ** -0.5)
        q_lo.append(jnp.where(lo, q, 0.0).astype(_BF))
        q_hi.append(jnp.where(lo, 0.0, q).astype(_BF))
    v = v_ref[...]
    outs = []
    for qs in (q_lo, q_hi):
        qm = jnp.concatenate(qs, axis=0)
        s = jnp.dot(qm, kt_ref[...], preferred_element_type=_F32)
        m = jnp.max(s, axis=-1, keepdims=True)
        p = jnp.exp(s - m)
        l = jnp.sum(p, axis=-1, keepdims=True)
        o = jnp.dot(p.astype(_BF), v, preferred_element_type=_F32)
        outs.append(o / l)
    for j in range(A_Q_W // LANES):
        o = jnp.where(lo, outs[0][j * tq:(j + 1) * tq], outs[1][j * tq:(j + 1) * tq])
        o_ref[:, j * LANES:(j + 1) * LANES] = o.astype(o_ref.dtype)


def _attn_a(proj, cos, sin, gq, gk, *, tq):
    b, s, _ = proj.shape
    return pl.pallas_call(
        functools.partial(_attn_a_kernel, tq=tq),
        grid=(b, s // tq),
        in_specs=[
            pl.BlockSpec((None, tq, A_Q_W), lambda bi, i: (bi, i, 0)),
            pl.BlockSpec((None, s, LANES), lambda bi, i: (bi, 0, A_Q_W // LANES)),
            pl.BlockSpec((None, s, LANES), lambda bi, i: (bi, 0, A_Q_W // LANES + 1)),
            pl.BlockSpec((tq, LANES), lambda bi, i: (i, 0)),
            pl.BlockSpec((tq, LANES), lambda bi, i: (i, 0)),
            pl.BlockSpec((s, LANES), lambda bi, i: (0, 0)),
            pl.BlockSpec((s, LANES), lambda bi, i: (0, 0)),
            pl.BlockSpec((1, LANES), lambda bi, i: (0, 0)),
            pl.BlockSpec((1, LANES), lambda bi, i: (0, 0)),
        ],
        out_specs=pl.BlockSpec((None, tq, A_Q_W), lambda bi, i: (bi, i, 0)),
        out_shape=jax.ShapeDtypeStruct((b, s, A_Q_W), _BF),
        scratch_shapes=[pltpu.VMEM((LANES, s), _BF)],
        compiler_params=_cparams("parallel", "arbitrary"),
        name="attn_a",
    )(proj, proj, proj, cos, sin, cos, sin, gq, gk)


def _attn_b_kernel(q_ref, k_ref, v_ref, c_ref, s_ref, o_ref, lse_ref, kr_ref, *, m_len, tq, win):
    n_slab = B_W // LANES
    half = ROPE_DIMS // 2
    for p in range(n_slab):
        sl = slice(p * LANES, (p + 1) * LANES)
        kr_ref[:, sl] = _rope_slab(k_ref[:, sl].astype(_F32), c_ref[...], s_ref[...], half).astype(_BF)

    lane = _lane((tq, LANES))
    lo = lane < HEAD_DIM
    row = lax.broadcasted_iota(jnp.int32, (tq, win), 0)
    col = lax.broadcasted_iota(jnp.int32, (tq, win), 1)

    def block(i):
        if isinstance(i, int):
            q0 = i * tq
            start = min(max(q0 - B_RADIUS, 0), m_len - win)
        else:
            q0 = pl.multiple_of(i * tq, tq)
            start = pl.multiple_of(jnp.clip(q0 - B_RADIUS, 0, m_len - win), B_RADIUS)
        dist = row - col + (q0 - start)
        valid = jnp.abs(dist) <= B_RADIUS
        cq = c_ref[pl.ds(q0, tq), :]
        sq = s_ref[pl.ds(q0, tq), :]
        for p in range(n_slab):
            sl = slice(p * LANES, (p + 1) * LANES)
            q = _rope_slab(q_ref[pl.ds(q0, tq), sl].astype(_F32), cq, sq, half) * (HEAD_DIM ** -0.5)
            kw = kr_ref[pl.ds(start, win), sl]
            vw = v_ref[pl.ds(start, win), sl]
            res = []
            for qm in (jnp.where(lo, q, 0.0), jnp.where(lo, 0.0, q)):
                s = lax.dot_general(qm.astype(_BF), kw, (((1,), (1,)), ((), ())), preferred_element_type=_F32)
                s = jnp.where(valid, s, NEG_INF)
                mx = jnp.max(s, axis=-1, keepdims=True)
                pr = jnp.exp(s - mx)
                l = jnp.sum(pr, axis=-1, keepdims=True)
                o = jnp.dot(pr.astype(_BF), vw, preferred_element_type=_F32) / l
                res.append((o, mx + jnp.log(l)))
            o_ref[pl.ds(q0, tq), sl] = jnp.where(lo, res[0][0], res[1][0]).astype(o_ref.dtype)
            lse_ref[pl.ds(q0, tq), sl] = jnp.where(lo, res[0][1], res[1][1])

    nb = m_len // tq
    if nb == 1:
        block(0)
    else:
        def body(i, c):
            block(i)
            return c
        lax.fori_loop(0, nb, body, 0)


def _attn_b(proj, cos, sin, group):
    b, s, cols = proj.shape
    d = B_DILATIONS[group]
    m_len = s // d
    tq = min(128, m_len)
    win = min(tq + 2 * B_RADIUS, m_len)
    cb = cols // B_W
    c0 = 2 + 3 * group
    pv = proj.reshape(b, m_len, d * cols)
    cv = cos.reshape(m_len, d * LANES)
    sv = sin.reshape(m_len, d * LANES)
    o, lse = pl.pallas_call(
        functools.partial(_attn_b_kernel, m_len=m_len, tq=tq, win=win),
        grid=(b, d),
        in_specs=[
            pl.BlockSpec((None, m_len, B_W), lambda bi, r: (bi, 0, r * cb + c0)),
            pl.BlockSpec((None, m_len, B_W), lambda bi, r: (bi, 0, r * cb + c0 + 1)),
            pl.BlockSpec((None, m_len, B_W), lambda bi, r: (bi, 0, r * cb + c0 + 2)),
            pl.BlockSpec((m_len, LANES), lambda bi, r: (0, r)),
            pl.BlockSpec((m_len, LANES), lambda bi, r: (0, r)),
        ],
        out_specs=[
            pl.BlockSpec((None, m_len, B_W), lambda bi, r: (bi, 0, r)),
            pl.BlockSpec((None, m_len, B_W), lambda bi, r: (bi, 0, r)),
        ],
        out_shape=[
            jax.ShapeDtypeStruct((b, m_len, d * B_W), _BF),
            jax.ShapeDtypeStruct((b, m_len, d * B_W), _F32),
        ],
        scratch_shapes=[pltpu.VMEM((m_len, B_W), _BF)],
        compiler_params=_cparams("parallel", "parallel"),
        name=f"attn_b{group}",
    )(pv, pv, pv, cv, sv)
    return o.reshape(b * s, B_W), lse.reshape(b * s, B_W)


def _even_out_kernel(x_ref, a_ref, o0_ref, o1_ref, o2_ref, l0_ref, l1_ref, l2_ref, w_ref, y_ref):
    l0, l1, l2 = l0_ref[...], l1_ref[...], l2_ref[...]
    mx = jnp.maximum(jnp.maximum(l0, l1), l2)
    e0, e1, e2 = jnp.exp(l0 - mx), jnp.exp(l1 - mx), jnp.exp(l2 - mx)
    bo = (e0 * o0_ref[...].astype(_F32) + e1 * o1_ref[...].astype(_F32) + e2 * o2_ref[...].astype(_F32)) / (e0 + e1 + e2)
    y = jnp.dot(a_ref[...], w_ref[:A_Q_W, :], preferred_element_type=_F32)
    y = y + jnp.dot(bo.astype(_BF), w_ref[A_Q_W:, :], preferred_element_type=_F32)
    y_ref[...] = x_ref[...] + y


def _even_out(x, a_o, outs, lses, w, *, tm):
    n, d = x.shape
    row = lambda width: pl.BlockSpec((tm, width), lambda i: (i, 0))
    return pl.pallas_call(
        _even_out_kernel,
        grid=(n // tm,),
        in_specs=[row(d), row(A_Q_W)] + [row(B_W)] * 6 + [pl.BlockSpec(w.shape, lambda i: (0, 0))],
        out_specs=row(d),
        out_shape=jax.ShapeDtypeStruct((n, d), _F32),
        compiler_params=_cparams("parallel"),
        name="even_out",
    )(x, a_o, *outs, *lses, w)


def _res_proj_kernel(x_ref, a_ref, w_ref, y_ref):
    y_ref[...] = x_ref[...] + jnp.dot(a_ref[...], w_ref[...], preferred_element_type=_F32)


def _res_proj(x, a, w, *, tm):
    n, d = x.shape
    return pl.pallas_call(
        _res_proj_kernel,
        grid=(n // tm,),
        in_specs=[
            pl.BlockSpec((tm, d), lambda i: (i, 0)),
            pl.BlockSpec((tm, a.shape[1]), lambda i: (i, 0)),
            pl.BlockSpec(w.shape, lambda i: (0, 0)),
        ],
        out_specs=pl.BlockSpec((tm, d), lambda i: (i, 0)),
        out_shape=jax.ShapeDtypeStruct((n, d), _F32),
        compiler_params=_cparams("parallel"),
        name="res_proj",
    )(x, a, w)


def _ffn_kernel(x_ref, g_ref, wg_ref, wu_ref, wd_ref, gf_ref, y_ref, h_ref, act_ref, *, chunk, final_norm):
    x = x_ref[...]
    h_ref[...] = _rms_rows(x, g_ref[...]).astype(_BF)
    for c in range(D_FF // chunk):
        sl = slice(c * chunk, (c + 1) * chunk)
        gate = jnp.dot(h_ref[...], wg_ref[:, sl], preferred_element_type=_F32)
        up = jnp.dot(h_ref[...], wu_ref[:, sl], preferred_element_type=_F32)
        act_ref[:, sl] = (gate * _sigmoid(gate) * up).astype(_BF)
    y = x + jnp.dot(act_ref[...], wd_ref[...], preferred_element_type=_F32)
    if final_norm:
        y = _rms_rows(y, gf_ref[...])
    y_ref[...] = y


def _ffn(x, g, w_gate_up, w_down, g_final, *, tm, final_norm):
    n, d = x.shape
    const = lambda shape, idx: pl.BlockSpec(shape, lambda i: idx, pipeline_mode=pl.Buffered(1))
    return pl.pallas_call(
        functools.partial(_ffn_kernel, chunk=256, final_norm=final_norm),
        grid=(n // tm,),
        in_specs=[
            pl.BlockSpec((tm, d), lambda i: (i, 0)),
            const((1, d), (0, 0)),
            const((d, D_FF), (0, 0)),
            const((d, D_FF), (0, 1)),
            const((D_FF, d), (0, 0)),
            const((1, d), (0, 0)),
        ],
        out_specs=pl.BlockSpec((tm, d), lambda i: (i, 0)),
        out_shape=jax.ShapeDtypeStruct((n, d), _F32),
        scratch_shapes=[pltpu.VMEM((tm, d), _BF), pltpu.VMEM((tm, D_FF), _BF)],
        compiler_params=_cparams("parallel"),
        name="ffn",
    )(x, g, w_gate_up, w_gate_up, w_down, g_final)


def _conv_kernel(u_ref, w_ref, b_ref, o_ref):
    s = u_ref.shape[0]
    t = lax.broadcasted_iota(jnp.int32, (s, LANES), 0)
    for c in range(D_RNN // LANES):
        sl = slice(c * LANES, (c + 1) * LANES)
        u = u_ref[:, sl]
        um1 = jnp.where(t >= 1, pltpu.roll(u, 1, 0), 0.0)
        up1 = jnp.where(t < s - 1, pltpu.roll(u, s - 1, 0), 0.0)
        up2 = jnp.where(t < s - 2, pltpu.roll(u, s - 2, 0), 0.0)
        w = w_ref[:, sl]
        o_ref[:, sl] = um1 * w[0:1] + u * w[1:2] + up1 * w[2:3] + up2 * w[3:4] + b_ref[:, sl]


def _conv(u, w, bias):
    b, s, d = u.shape
    return pl.pallas_call(
        _conv_kernel,
        grid=(b,),
        in_specs=[
            pl.BlockSpec((None, s, d), lambda i: (i, 0, 0)),
            pl.BlockSpec(w.shape, lambda i: (0, 0)),
            pl.BlockSpec((1, d), lambda i: (0, 0)),
        ],
        out_specs=pl.BlockSpec((None, s, d), lambda i: (i, 0, 0)),
        out_shape=jax.ShapeDtypeStruct((b, s, d), _F32),
        compiler_params=_cparams("parallel"),
        name="conv",
    )(u, w, bias)


SCAN_BATCH = 8


def _scan_kernel(*refs, tc, reverse, finalize):
    if finalize:
        u_ref, w_ref, ba_ref, bx_ref, lam_ref, hf_ref, gate_ref, o_ref, a_s, b_s, carry = refs
    else:
        u_ref, w_ref, ba_ref, bx_ref, lam_ref, o_ref, a_s, b_s, carry = refs

    @pl.when(pl.program_id(1) == 0)
    def _():
        carry[...] = jnp.zeros_like(carry)

    for n in range(LRU_BLOCKS):
        sl = slice(n * LRU_BW, (n + 1) * LRU_BW)
        lam = lam_ref[:, sl]
        neg_sp = -(jnp.maximum(-lam, 0.0) + jnp.log1p(jnp.exp(-jnp.abs(lam))))
        for bi in range(SCAN_BATCH):
            x = u_ref[bi, :, sl]
            z = jnp.dot(x.astype(_BF), w_ref[n], preferred_element_type=_F32)
            r = _sigmoid(z[:, :LRU_BW] + ba_ref[:, sl])
            ig = _sigmoid(z[:, LRU_BW:] + bx_ref[:, sl])
            a = jnp.exp(LRU_C * r * neg_sp)
            a_s[n, bi * tc:(bi + 1) * tc, :] = a
            b_s[n, bi * tc:(bi + 1) * tc, :] = jnp.sqrt(1.0 - a * a) * (ig * x)

    def step(i, hs):
        t = (tc - 1 - i) if reverse else i
        new = []
        for n in range(LRU_BLOCKS):
            idx = pl.ds(t, SCAN_BATCH, stride=tc)
            h = a_s[n, idx, :] * hs[n] + b_s[n, idx, :]
            a_s[n, idx, :] = h
            new.append(h)
        return tuple(new)

    hs = lax.fori_loop(0, tc, step, tuple(carry[n] for n in range(LRU_BLOCKS)))
    for n in range(LRU_BLOCKS):
        carry[n] = hs[n]

    for n in range(LRU_BLOCKS):
        sl = slice(n * LRU_BW, (n + 1) * LRU_BW)
        for bi in range(SCAN_BATCH):
            h = a_s[n, bi * tc:(bi + 1) * tc, :]
            if finalize:
                o_ref[bi, :, sl] = ((hf_ref[bi, :, sl] + h) * gate_ref[bi, :, sl].astype(_F32)).astype(o_ref.dtype)
            else:
                o_ref[bi, :, sl] = h


def _scan(u, w, ba, bx, lam, *, tc, reverse, h_fwd=None, gate=None):
    b, s, d = u.shape
    nt = s // tc
    finalize = h_fwd is not None
    tmap = (lambda g, c: (g, nt - 1 - c, 0)) if reverse else (lambda g, c: (g, c, 0))
    blk = pl.BlockSpec((SCAN_BATCH, tc, d), tmap)
    vec = pl.BlockSpec((1, d), lambda g, c: (0, 0))
    in_specs = [blk, pl.BlockSpec(w.shape, lambda g, c: (0, 0, 0)), vec, vec, vec]
    args = [u, w, ba, bx, lam]
    if finalize:
        in_specs += [blk, blk]
        args += [h_fwd, gate]
    return pl.pallas_call(
        functools.partial(_scan_kernel, tc=tc, reverse=reverse, finalize=finalize),
        grid=(b // SCAN_BATCH, nt),
        in_specs=in_specs,
        out_specs=blk,
        out_shape=jax.ShapeDtypeStruct((b, s, d), _BF if finalize else _F32),
        scratch_shapes=[
            pltpu.VMEM((LRU_BLOCKS, SCAN_BATCH * tc, LRU_BW), _F32),
            pltpu.VMEM((LRU_BLOCKS, SCAN_BATCH * tc, LRU_BW), _F32),
            pltpu.VMEM((LRU_BLOCKS, SCAN_BATCH, LRU_BW), _F32),
        ],
        compiler_params=_cparams("parallel", "arbitrary"),
        name="scan_bwd" if reverse else "scan_fwd",
    )(*args)


def _rope_tables(pos_lo, pos_hi, n_rot, theta, s):
    half = n_rot // 2
    inv = jnp.power(jnp.float32(theta), -jnp.arange(half, dtype=_F32) * (2.0 / n_rot))

    def part(pos):
        ang = pos.astype(_F32)[:, None] * inv[None, :]
        c, sn = jnp.cos(ang), jnp.sin(ang)
        return jnp.concatenate([c, c], axis=-1), jnp.concatenate([-sn, sn], axis=-1)

    cs, sns = [], []
    for pos in (pos_lo, pos_hi):
        if pos is not None:
            c, sn = part(pos)
            cs.append(c)
            sns.append(sn)
    width = sum(c.shape[-1] for c in cs)
    cs.append(jnp.ones((s, HEAD_DIM - width), _F32))
    sns.append(jnp.zeros((s, HEAD_DIM - width), _F32))
    c = jnp.concatenate(cs, axis=-1)
    sn = jnp.concatenate(sns, axis=-1)
    return jnp.concatenate([c, c], axis=-1), jnp.concatenate([sn, sn], axis=-1)


def _even_w_in_layout(w):
    d = w.shape[0]
    aq = w[:, :A_Q_W].reshape(d, A_Q_HEADS, HEAD_DIM)[:, jnp.array(A_HEAD_ORDER), :].reshape(d, A_Q_W)
    akv = w[:, A_Q_W:A_Q_W + 2 * LANES]
    pad = jnp.zeros((d, 1024 - A_Q_W - 2 * LANES), w.dtype)
    return jnp.concatenate([aq, akv, pad, w[:, A_Q_W + 2 * LANES:]], axis=1).astype(_BF)


def _even_w_out_layout(w):
    d = w.shape[1]
    wa = w[:A_Q_W].reshape(A_Q_HEADS, HEAD_DIM, d)[jnp.array(A_HEAD_ORDER)].reshape(A_Q_W, d)
    return jnp.concatenate([wa, w[A_Q_W:]], axis=0).astype(_BF)


def kernel(x, mix_norm, ffn_norm, final_norm, even_w_in, even_q_norm, even_k_norm, even_w_out, odd_w_in, odd_conv_w, odd_conv_b, odd_gate_a_w, odd_gate_a_b, odd_gate_x_w, odd_gate_x_b, odd_lambda, odd_w_out, ffn_w_gate_up, ffn_w_down):
    b, s, d = x.shape
    n = b * s
    depth = mix_norm.shape[0]
    assert d == D_MODEL and s % (B_DILATIONS[-1] * 8) == 0 and b % SCAN_BATCH == 0

    pos = jnp.arange(s, dtype=jnp.int32)
    cos_a, sin_a = _rope_tables(pos // GRID_W, pos % GRID_W, HEAD_DIM // 2, AXIAL_THETA, s)
    cos_b, sin_b = _rope_tables(pos, None, ROPE_DIMS, ROPE_THETA, s)

    tm = 1024 if n % 1024 == 0 else s
    xf = x.reshape(n, d)
    for layer in range(depth):
        j = layer // 2
        g_mix = mix_norm[layer].reshape(1, d)
        if layer % 2 == 0:
            proj = _norm_proj(xf, g_mix, _even_w_in_layout(even_w_in[j]), tm=tm, tn=512)
            proj3 = proj.reshape(b, s, EVEN_COLS)
            gq = jnp.tile(even_q_norm[j], 2).reshape(1, LANES)
            gk = jnp.tile(even_k_norm[j], 2).reshape(1, LANES)
            a_o = _attn_a(proj3, cos_a, sin_a, gq, gk, tq=128).reshape(n, A_Q_W)
            outs, lses = zip(*[_attn_b(proj3, cos_b, sin_b, grp) for grp in range(len(B_DILATIONS))])
            xf = _even_out(xf, a_o, outs, lses, _even_w_out_layout(even_w_out[j]), tm=512)
        else:
            gate, u = _odd_in(xf, g_mix, odd_w_in[j].astype(_BF), tm=tm)
            uc = _conv(u.reshape(b, s, D_RNN), odd_conv_w[j], odd_conv_b[j].reshape(1, D_RNN))
            h = None
            for direction in range(2):
                w = jnp.concatenate([odd_gate_a_w[j, direction], odd_gate_x_w[j, direction]], axis=-1).astype(_BF)
                h = _scan(uc, w, odd_gate_a_b[j, direction].reshape(1, D_RNN), odd_gate_x_b[j, direction].reshape(1, D_RNN),
                          odd_lambda[j, direction].reshape(1, D_RNN), tc=128, reverse=direction == 1,
                          h_fwd=h, gate=None if direction == 0 else gate.reshape(b, s, D_RNN))
            xf = _res_proj(xf, h.reshape(n, D_RNN), odd_w_out[j].astype(_BF), tm=tm)
        xf = _ffn(xf, ffn_norm[layer].reshape(1, d), ffn_w_gate_up[layer].astype(_BF), ffn_w_down[layer].astype(_BF),
                  final_norm.reshape(1, d), tm=512, final_norm=layer == depth - 1)
    return xf.reshape(b, s, d)
```

```python
import functools

import jax
import jax.numpy as jnp
from jax import lax
from jax.experimental import pallas as pl
from jax.experimental.pallas import tpu as pltpu

D_MODEL = 1024
HEAD_DIM = 64
A_Q_HEADS = 8
A_KV_HEADS = 2
B_HEADS = 8
B_RADIUS = 64
GRID_W = 64
AXIAL_THETA = 10000.0
ROPE_THETA = 500000.0
ROPE_DIMS = HEAD_DIM // 4
D_RNN = D_MODEL
LRU_BLOCKS = 8
LRU_BW = D_RNN // LRU_BLOCKS
LRU_C = 8.0
D_FF = 2816
EPS = 1e-6
NEG_INF = -1e30
LOG2E = 1.4426950408889634

LANES = 128
A_Q_W = A_Q_HEADS * HEAD_DIM
B_W = B_HEADS * HEAD_DIM
N_SLAB = B_W // LANES
A_HEAD_ORDER = (0, 4, 1, 5, 2, 6, 3, 7)
NAT_COLS = 1024 + 3 * B_W
PERM_COLS = 2 * 3 * B_W
PERM = 16
PROJ_CHUNK = 512

VMEM_LIMIT = 56 * 1024 * 1024

_BF = jnp.bfloat16
_F32 = jnp.float32


def _cparams(*sem):
    return pltpu.CompilerParams(dimension_semantics=sem, vmem_limit_bytes=VMEM_LIMIT)


def _const_spec(shape):
    return pl.BlockSpec(shape, lambda *_: (0,) * len(shape), pipeline_mode=pl.Buffered(1))


def _rms_rows(x, g):
    ms = jnp.mean(x * x, axis=-1, keepdims=True)
    return x * lax.rsqrt(ms + EPS) * g


def _sigmoid(x):
    return 0.5 * (1.0 + jnp.tanh(0.5 * x))


def _lane(shape):
    return lax.broadcasted_iota(jnp.int32, shape, len(shape) - 1)


def _to_residue_major(y):
    rows, c = y.shape
    return jnp.swapaxes(y.reshape(rows // PERM, PERM, c), 0, 1).reshape(rows, c)


def _from_residue_major(y):
    p, m, c = y.shape
    return jnp.swapaxes(y, 0, 1).reshape(p * m, c)


def _even_in_kernel(x_ref, g_ref, wn_ref, wp_ref, nat_ref, perm_ref, hn_ref, hp_ref):
    tm = x_ref.shape[0]
    y = _rms_rows(x_ref[...], g_ref[...])
    hn_ref[...] = y.astype(_BF)
    hp_ref[...] = _to_residue_major(y).astype(_BF)
    for c in range(wn_ref.shape[1] // PROJ_CHUNK):
        sl = slice(c * PROJ_CHUNK, (c + 1) * PROJ_CHUNK)
        nat_ref[:, sl] = jnp.dot(hn_ref[...], wn_ref[:, sl], preferred_element_type=_F32).astype(_BF)
    for c in range(wp_ref.shape[1] // PROJ_CHUNK):
        sl = slice(c * PROJ_CHUNK, (c + 1) * PROJ_CHUNK)
        acc = jnp.dot(hp_ref[...], wp_ref[:, sl], preferred_element_type=_F32)
        perm_ref[:, :, sl] = acc.reshape(PERM, tm // PERM, PROJ_CHUNK).astype(_BF)


def _even_in(x, g, w_nat, w_perm, *, b, s, tm):
    n, d = x.shape
    tpb = s // tm
    cn, cp = w_nat.shape[1], w_perm.shape[1]
    return pl.pallas_call(
        _even_in_kernel,
        grid=(n // tm,),
        in_specs=[pl.BlockSpec((tm, d), lambda i: (i, 0)), _const_spec((1, d)), _const_spec((d, cn)), _const_spec((d, cp))],
        out_specs=[
            pl.BlockSpec((tm, cn), lambda i: (i, 0)),
            pl.BlockSpec((None, PERM, tm // PERM, cp), lambda i: (i // tpb, 0, i % tpb, 0)),
        ],
        out_shape=[
            jax.ShapeDtypeStruct((n, cn), _BF),
            jax.ShapeDtypeStruct((b, PERM, s // PERM, cp), _BF),
        ],
        scratch_shapes=[pltpu.VMEM((tm, d), _BF), pltpu.VMEM((tm, d), _BF)],
        compiler_params=_cparams("parallel"),
        name="even_in",
    )(x, g, w_nat, w_perm)


def _gelu_tanh(x):
    return 0.5 * x * (1.0 + jnp.tanh(0.7978845608028654 * (x + 0.044715 * (x * x * x))))


def _odd_in_kernel(x_ref, g_ref, w_ref, gate_ref, u_ref, h_ref):
    h_ref[...] = _rms_rows(x_ref[...], g_ref[...]).astype(_BF)
    for c in range(D_RNN // PROJ_CHUNK):
        sl = slice(c * PROJ_CHUNK, (c + 1) * PROJ_CHUNK)
        acc = jnp.dot(h_ref[...], w_ref[:, sl], preferred_element_type=_F32)
        gate_ref[:, sl] = _gelu_tanh(acc).astype(gate_ref.dtype)
    for c in range(D_RNN // PROJ_CHUNK):
        sl = slice(c * PROJ_CHUNK, (c + 1) * PROJ_CHUNK)
        u_ref[:, sl] = jnp.dot(h_ref[...], w_ref[:, D_RNN + c * PROJ_CHUNK:D_RNN + (c + 1) * PROJ_CHUNK],
                               preferred_element_type=_F32)


def _odd_in(x, g, w, *, tm):
    n, d = x.shape
    return pl.pallas_call(
        _odd_in_kernel,
        grid=(n // tm,),
        in_specs=[pl.BlockSpec((tm, d), lambda i: (i, 0)), _const_spec((1, d)), _const_spec(w.shape)],
        out_specs=[pl.BlockSpec((tm, D_RNN), lambda i: (i, 0)), pl.BlockSpec((tm, D_RNN), lambda i: (i, 0))],
        out_shape=[jax.ShapeDtypeStruct((n, D_RNN), _BF), jax.ShapeDtypeStruct((n, D_RNN), _F32)],
        scratch_shapes=[pltpu.VMEM((tm, d), _BF)],
        compiler_params=_cparams("parallel"),
        name="odd_in",
    )(x, g, w)


def _rope_slab(x, cos, sin_signed, half):
    lane = _lane(x.shape)
    fwd = pltpu.roll(x, LANES - half, 1)
    bwd = pltpu.roll(x, half, 1)
    partner = jnp.where((lane % (2 * half)) < half, fwd, bwd)
    return x * cos + partner * sin_signed


def _head_rms_slab(x, g):
    lo = _lane(x.shape) < HEAD_DIM
    sq = x * x
    ms_lo = jnp.sum(jnp.where(lo, sq, 0.0), axis=-1, keepdims=True) * (1.0 / HEAD_DIM)
    ms_hi = jnp.sum(jnp.where(lo, 0.0, sq), axis=-1, keepdims=True) * (1.0 / HEAD_DIM)
    rs = jnp.where(lo, lax.rsqrt(ms_lo + EPS), lax.rsqrt(ms_hi + EPS))
    return x * rs * g


def _attn_a_kernel(q_ref, k_ref, v_ref, cq_ref, sq_ref, ck_ref, sk_ref, gq_ref, gk_ref, o_ref, kt_ref, *, tq):
    @pl.when(pl.program_id(1) == 0)
    def _():
        k = _head_rms_slab(k_ref[...].astype(_F32), gk_ref[...])
        k = _rope_slab(k, ck_ref[...], sk_ref[...], 16)
        kt_ref[...] = k.T.astype(_BF)

    lane = _lane((tq, LANES))
    lo = lane < HEAD_DIM
    cq = cq_ref[...]
    sq = sq_ref[...]
    gq = gq_ref[...]
    q_lo, q_hi = [], []
    for j in range(A_Q_W // LANES):
        q = q_ref[:, j * LANES:(j + 1) * LANES].astype(_F32)
        q = _rope_slab(_head_rms_slab(q, gq), cq, sq, 16) * (HEAD_DIM ** -0.5 * LOG2E)
        q_lo.append(jnp.where(lo, q, 0.0).astype(_BF))
        q_hi.append(jnp.where(lo, 0.0, q).astype(_BF))
    v = v_ref[...]
    outs = []
    for qs in (q_lo, q_hi):
        qm = jnp.concatenate(qs, axis=0)
        s = jnp.dot(qm, kt_ref[...], preferred_element_type=_F32)
        m = jnp.max(s, axis=-1, keepdims=True)
        p = jnp.exp2(s - m)
        l = jnp.sum(p, axis=-1, keepdims=True)
        o = jnp.dot(p.astype(_BF), v, preferred_element_type=_F32)
        outs.append(o / l)
    for j in range(A_Q_W // LANES):
        o = jnp.where(lo, outs[0][j * tq:(j + 1) * tq], outs[1][j * tq:(j + 1) * tq])
        o_ref[:, j * LANES:(j + 1) * LANES] = o.astype(o_ref.dtype)


def _attn_a(nat, cos, sin, gq, gk, *, tq):
    b, s, _ = nat.shape
    return pl.pallas_call(
        functools.partial(_attn_a_kernel, tq=tq),
        grid=(b, s // tq),
        in_specs=[
            pl.BlockSpec((None, tq, A_Q_W), lambda bi, i: (bi, i, 0)),
            pl.BlockSpec((None, s, LANES), lambda bi, i: (bi, 0, A_Q_W // LANES)),
            pl.BlockSpec((None, s, LANES), lambda bi, i: (bi, 0, A_Q_W // LANES + 1)),
            pl.BlockSpec((tq, LANES), lambda bi, i: (i, 0)),
            pl.BlockSpec((tq, LANES), lambda bi, i: (i, 0)),
            pl.BlockSpec((s, LANES), lambda bi, i: (0, 0)),
            pl.BlockSpec((s, LANES), lambda bi, i: (0, 0)),
            pl.BlockSpec((1, LANES), lambda bi, i: (0, 0)),
            pl.BlockSpec((1, LANES), lambda bi, i: (0, 0)),
        ],
        out_specs=pl.BlockSpec((None, tq, A_Q_W), lambda bi, i: (bi, i, 0)),
        out_shape=jax.ShapeDtypeStruct((b, s, A_Q_W), _BF),
        scratch_shapes=[pltpu.VMEM((LANES, s), _BF)],
        compiler_params=_cparams("parallel", "arbitrary"),
        name="attn_a",
    )(nat, nat, nat, cos, sin, cos, sin, gq, gk)


def _band_slab(q, kw, vw, valid):
    tq = q.shape[0]
    lo = _lane(q.shape) < HEAD_DIM
    qm = jnp.concatenate([jnp.where(lo, q, 0.0), jnp.where(lo, 0.0, q)], axis=0).astype(_BF)
    s = lax.dot_general(qm, kw, (((1,), (1,)), ((), ())), preferred_element_type=_F32)
    s = jnp.where(jnp.concatenate([valid, valid], axis=0), s, NEG_INF)
    mx = jnp.max(s, axis=-1, keepdims=True)
    pr = jnp.exp(s - mx)
    l = jnp.sum(pr, axis=-1, keepdims=True)
    o = jnp.dot(pr.astype(_BF), vw, preferred_element_type=_F32) / l
    lse = mx + jnp.log(l)
    return jnp.where(lo, o[:tq], o[tq:]), jnp.where(lo, lse[:tq], lse[tq:])


def _attn_b1_kernel(q_ref, k_ref, v_ref, c_ref, s_ref, o_ref, lse_ref, kr_ref, *, m_len, tq, win):
    half = ROPE_DIMS // 2
    for p in range(N_SLAB):
        sl = slice(p * LANES, (p + 1) * LANES)
        kr_ref[:, sl] = _rope_slab(k_ref[:, sl].astype(_F32), c_ref[...], s_ref[...], half).astype(_BF)

    row = lax.broadcasted_iota(jnp.int32, (tq, win), 0)
    col = lax.broadcasted_iota(jnp.int32, (tq, win), 1)

    def block(i, carry):
        q0 = pl.multiple_of(i * tq, tq)
        start = pl.multiple_of(jnp.clip(q0 - B_RADIUS, 0, m_len - win), B_RADIUS)
        valid = jnp.abs(row - col + (q0 - start)) <= B_RADIUS
        cq = c_ref[pl.ds(q0, tq), :]
        sq = s_ref[pl.ds(q0, tq), :]
        for p in range(N_SLAB):
            sl = slice(p * LANES, (p + 1) * LANES)
            q = _rope_slab(q_ref[pl.ds(q0, tq), sl].astype(_F32), cq, sq, half) * (HEAD_DIM ** -0.5)
            o, lse = _band_slab(q, kr_ref[pl.ds(start, win), sl], v_ref[pl.ds(start, win), sl], valid)
            o_ref[pl.ds(q0, tq), sl] = o.astype(o_ref.dtype)
            lse_ref[pl.ds(q0, tq), sl] = lse
        return carry

    lax.fori_loop(0, m_len // tq, block, 0)


def _attn_b1(nat, cos, sin):
    b, s, _ = nat.shape
    tq = 128
    win = min(tq + 2 * B_RADIUS, s)
    c0 = 1024 // B_W
    o, lse = pl.pallas_call(
        functools.partial(_attn_b1_kernel, m_len=s, tq=tq, win=win),
        grid=(b,),
        in_specs=[
            pl.BlockSpec((None, s, B_W), lambda bi: (bi, 0, c0)),
            pl.BlockSpec((None, s, B_W), lambda bi: (bi, 0, c0 + 1)),
            pl.BlockSpec((None, s, B_W), lambda bi: (bi, 0, c0 + 2)),
            _const_spec((s, LANES)),
            _const_spec((s, LANES)),
        ],
        out_specs=[pl.BlockSpec((None, s, B_W), lambda bi: (bi, 0, 0))] * 2,
        out_shape=[jax.ShapeDtypeStruct((b, s, B_W), _BF), jax.ShapeDtypeStruct((b, s, B_W), _F32)],
        scratch_shapes=[pltpu.VMEM((s, B_W), _BF)],
        compiler_params=_cparams("parallel"),
        name="attn_b1",
    )(nat, nat, nat, cos, sin)
    return o.reshape(b * s, B_W), lse.reshape(b * s, B_W)


def _attn_b4_kernel(q_ref, k_ref, v_ref, c_ref, s_ref, o_ref, lse_ref, kr_ref, *, m16):
    half = ROPE_DIMS // 2
    seg = 4
    tqs = 32
    wks = min(tqs + 2 * (B_RADIUS // seg), m16)
    for j in range(seg):
        for p in range(N_SLAB):
            sl = slice(p * LANES, (p + 1) * LANES)
            kr_ref[j, :, sl] = _rope_slab(k_ref[j, :, sl].astype(_F32), c_ref[j], s_ref[j], half).astype(_BF)

    tq, win = seg * tqs, seg * wks
    row = lax.broadcasted_iota(jnp.int32, (tq, win), 0)
    col = lax.broadcasted_iota(jnp.int32, (tq, win), 1)
    for i in range(m16 // tqs):
        a_q = i * tqs
        a_k = min(max(a_q - B_RADIUS // seg, 0), m16 - wks)
        qpos = seg * (a_q + (row & (tqs - 1))) + (row >> (tqs.bit_length() - 1))
        kpos = seg * (a_k + (col & (wks - 1))) + (col >> (wks.bit_length() - 1))
        valid = jnp.abs(qpos - kpos) <= B_RADIUS
        cq = jnp.concatenate([c_ref[j, a_q:a_q + tqs, :] for j in range(seg)], axis=0)
        sq = jnp.concatenate([s_ref[j, a_q:a_q + tqs, :] for j in range(seg)], axis=0)
        for p in range(N_SLAB):
            sl = slice(p * LANES, (p + 1) * LANES)
            q = jnp.concatenate([q_ref[j, a_q:a_q + tqs, sl] for j in range(seg)], axis=0).astype(_F32)
            q = _rope_slab(q, cq, sq, half) * (HEAD_DIM ** -0.5)
            kw = jnp.concatenate([kr_ref[j, a_k:a_k + wks, sl] for j in range(seg)], axis=0)
            vw = jnp.concatenate([v_ref[j, a_k:a_k + wks, sl] for j in range(seg)], axis=0)
            o, lse = _band_slab(q, kw, vw, valid)
            for j in range(seg):
                o_ref[j, a_q:a_q + tqs, sl] = o[j * tqs:(j + 1) * tqs].astype(o_ref.dtype)
                lse_ref[j, a_q:a_q + tqs, sl] = lse[j * tqs:(j + 1) * tqs]


def _attn_b4(perm, cos_p, sin_p):
    b, _, m16, cols = perm.shape
    pv = perm.reshape(b, 4, 4, m16, cols)
    cv = cos_p.reshape(4, 4, m16, LANES)
    sv = sin_p.reshape(4, 4, m16, LANES)
    blk = lambda cb: pl.BlockSpec((None, 4, None, m16, B_W), lambda bi, r: (bi, 0, r, 0, cb))
    tab = pl.BlockSpec((4, None, m16, LANES), lambda bi, r: (0, r, 0, 0))
    o, lse = pl.pallas_call(
        functools.partial(_attn_b4_kernel, m16=m16),
        grid=(b, 4),
        in_specs=[blk(0), blk(1), blk(2), tab, tab],
        out_specs=[blk(0), blk(0)],
        out_shape=[jax.ShapeDtypeStruct((b, 4, 4, m16, B_W), _BF), jax.ShapeDtypeStruct((b, 4, 4, m16, B_W), _F32)],
        scratch_shapes=[pltpu.VMEM((4, m16, B_W), _BF)],
        compiler_params=_cparams("parallel", "parallel"),
        name="attn_b4",
    )(pv, pv, pv, cv, sv)
    return o.reshape(b, PERM, m16, B_W), lse.reshape(b, PERM, m16, B_W)


def _attn_b16_kernel(q_ref, k_ref, v_ref, c_ref, s_ref, o_ref, lse_ref, *, m16, tq, win):
    half = ROPE_DIMS // 2
    row = lax.broadcasted_iota(jnp.int32, (tq, win), 0)
    col = lax.broadcasted_iota(jnp.int32, (tq, win), 1)

    def residue(r, carry):
        for i in range(m16 // tq):
            q0 = i * tq
            start = min(max(q0 - B_RADIUS, 0), m16 - win)
            valid = jnp.abs(row - col + (q0 - start)) <= B_RADIUS
            for p in range(N_SLAB):
                sl = slice(p * LANES, (p + 1) * LANES)
                q = _rope_slab(q_ref[r, q0:q0 + tq, sl].astype(_F32), c_ref[r, q0:q0 + tq, :], s_ref[r, q0:q0 + tq, :], half)
                kw = _rope_slab(k_ref[r, start:start + win, sl].astype(_F32), c_ref[r, start:start + win, :],
                                s_ref[r, start:start + win, :], half).astype(_BF)
                o, lse = _band_slab(q * (HEAD_DIM ** -0.5), kw, v_ref[r, start:start + win, sl], valid)
                o_ref[r, q0:q0 + tq, sl] = o.astype(o_ref.dtype)
                lse_ref[r, q0:q0 + tq, sl] = lse
        return carry

    lax.fori_loop(0, PERM, residue, 0, unroll=2)


def _attn_b16(perm, cos_p, sin_p):
    b, _, m16, cols = perm.shape
    tq = min(128, m16)
    win = min(tq + 2 * B_RADIUS, m16)
    c0 = 3
    blk = lambda cb: pl.BlockSpec((None, PERM, m16, B_W), lambda bi: (bi, 0, 0, cb))
    return pl.pallas_call(
        functools.partial(_attn_b16_kernel, m16=m16, tq=tq, win=win),
        grid=(b,),
        in_specs=[blk(c0), blk(c0 + 1), blk(c0 + 2), _const_spec((PERM, m16, LANES)), _const_spec((PERM, m16, LANES))],
        out_specs=[blk(0), blk(0)],
        out_shape=[jax.ShapeDtypeStruct((b, PERM, m16, B_W), _BF), jax.ShapeDtypeStruct((b, PERM, m16, B_W), _F32)],
        compiler_params=_cparams("parallel"),
        name="attn_b16",
    )(perm, perm, perm, cos_p, sin_p)


def _even_out_kernel(x_ref, a_ref, o0_ref, l0_ref, o1_ref, l1_ref, o2_ref, l2_ref, w_ref, y_ref):
    l0 = l0_ref[...]
    l1 = _from_residue_major(l1_ref[...])
    l2 = _from_residue_major(l2_ref[...])
    o0 = o0_ref[...].astype(_F32)
    o1 = _from_residue_major(o1_ref[...].astype(_F32))
    o2 = _from_residue_major(o2_ref[...].astype(_F32))
    mx = jnp.maximum(jnp.maximum(l0, l1), l2)
    e0, e1, e2 = jnp.exp(l0 - mx), jnp.exp(l1 - mx), jnp.exp(l2 - mx)
    bo = (e0 * o0 + e1 * o1 + e2 * o2) / (e0 + e1 + e2)
    y = jnp.dot(a_ref[...], w_ref[:A_Q_W, :], preferred_element_type=_F32)
    y = y + jnp.dot(bo.astype(_BF), w_ref[A_Q_W:, :], preferred_element_type=_F32)
    y_ref[...] = x_ref[...] + y


def _even_out(x, a_o, o0, l0, o1, l1, o2, l2, w, *, s, tm):
    n, d = x.shape
    tpb = s // tm
    row = lambda width: pl.BlockSpec((tm, width), lambda i: (i, 0))
    prm = pl.BlockSpec((None, PERM, tm // PERM, B_W), lambda i: (i // tpb, 0, i % tpb, 0))
    return pl.pallas_call(
        _even_out_kernel,
        grid=(n // tm,),
        in_specs=[row(d), row(A_Q_W), row(B_W), row(B_W), prm, prm, prm, prm, _const_spec(w.shape)],
        out_specs=row(d),
        out_shape=jax.ShapeDtypeStruct((n, d), _F32),
        compiler_params=_cparams("parallel"),
        name="even_out",
    )(x, a_o, o0, l0, o1, l1, o2, l2, w)


def _res_proj_kernel(x_ref, a_ref, w_ref, y_ref):
    y_ref[...] = x_ref[...] + jnp.dot(a_ref[...], w_ref[...], preferred_element_type=_F32)


def _res_proj(x, a, w, *, tm):
    n, d = x.shape
    return pl.pallas_call(
        _res_proj_kernel,
        grid=(n // tm,),
        in_specs=[pl.BlockSpec((tm, d), lambda i: (i, 0)), pl.BlockSpec((tm, a.shape[1]), lambda i: (i, 0)), _const_spec(w.shape)],
        out_specs=pl.BlockSpec((tm, d), lambda i: (i, 0)),
        out_shape=jax.ShapeDtypeStruct((n, d), _F32),
        compiler_params=_cparams("parallel"),
        name="res_proj",
    )(x, a, w)


def _ffn_kernel(x_ref, g_ref, wg_ref, wu_ref, wd_ref, gf_ref, y_ref, h_ref, act_ref, *, chunk, final_norm):
    x = x_ref[...]
    h_ref[...] = _rms_rows(x, g_ref[...]).astype(_BF)
    for c in range(D_FF // chunk):
        sl = slice(c * chunk, (c + 1) * chunk)
        gate = jnp.dot(h_ref[...], wg_ref[:, sl], preferred_element_type=_F32)
        up = jnp.dot(h_ref[...], wu_ref[:, sl], preferred_element_type=_F32)
        act_ref[:, sl] = (gate * _sigmoid(gate) * up).astype(_BF)
    y = x + jnp.dot(act_ref[...], wd_ref[...], preferred_element_type=_F32)
    if final_norm:
        y = _rms_rows(y, gf_ref[...])
    y_ref[...] = y


def _ffn(x, g, w_gate_up, w_down, g_final, *, tm, final_norm):
    n, d = x.shape
    const = lambda shape, idx: pl.BlockSpec(shape, lambda i: idx, pipeline_mode=pl.Buffered(1))
    return pl.pallas_call(
        functools.partial(_ffn_kernel, chunk=256, final_norm=final_norm),
        grid=(n // tm,),
        in_specs=[
            pl.BlockSpec((tm, d), lambda i: (i, 0)),
            const((1, d), (0, 0)),
            const((d, D_FF), (0, 0)),
            const((d, D_FF), (0, 1)),
            const((D_FF, d), (0, 0)),
            const((1, d), (0, 0)),
        ],
        out_specs=pl.BlockSpec((tm, d), lambda i: (i, 0)),
        out_shape=jax.ShapeDtypeStruct((n, d), _F32),
        scratch_shapes=[pltpu.VMEM((tm, d), _BF), pltpu.VMEM((tm, D_FF), _BF)],
        compiler_params=_cparams("parallel"),
        name="ffn",
    )(x, g, w_gate_up, w_gate_up, w_down, g_final)


def _conv_kernel(u_ref, w_ref, b_ref, o_ref):
    s = u_ref.shape[0]
    t = lax.broadcasted_iota(jnp.int32, (s, LANES), 0)
    for c in range(D_RNN // LANES):
        sl = slice(c * LANES, (c + 1) * LANES)
        u = u_ref[:, sl]
        um1 = jnp.where(t >= 1, pltpu.roll(u, 1, 0), 0.0)
        up1 = jnp.where(t < s - 1, pltpu.roll(u, s - 1, 0), 0.0)
        up2 = jnp.where(t < s - 2, pltpu.roll(u, s - 2, 0), 0.0)
        w = w_ref[:, sl]
        o_ref[:, sl] = um1 * w[0:1] + u * w[1:2] + up1 * w[2:3] + up2 * w[3:4] + b_ref[:, sl]


def _conv(u, w, bias):
    b, s, d = u.shape
    return pl.pallas_call(
        _conv_kernel,
        grid=(b,),
        in_specs=[pl.BlockSpec((None, s, d), lambda i: (i, 0, 0)), _const_spec(w.shape), _const_spec((1, d))],
        out_specs=pl.BlockSpec((None, s, d), lambda i: (i, 0, 0)),
        out_shape=jax.ShapeDtypeStruct((b, s, d), _F32),
        compiler_params=_cparams("parallel"),
        name="conv",
    )(u, w, bias)


SCAN_BATCH = 8
SCAN_UNROLL = 4


def _scan_kernel(*refs, tc, reverse, finalize):
    if finalize:
        u_ref, w_ref, ba_ref, bx_ref, lam_ref, hf_ref, gate_ref, o_ref, a_s, b_s, h_s, carry = refs
    else:
        u_ref, w_ref, ba_ref, bx_ref, lam_ref, o_ref, a_s, b_s, h_s, carry = refs
    rows = SCAN_BATCH * tc

    @pl.when(pl.program_id(1) == 0)
    def _():
        carry[...] = jnp.zeros_like(carry)

    for n in range(LRU_BLOCKS):
        sl = slice(n * LRU_BW, (n + 1) * LRU_BW)
        lam = lam_ref[:, sl]
        c2 = (-4.0 * LOG2E) * (jnp.maximum(-lam, 0.0) + jnp.log1p(jnp.exp(-jnp.abs(lam))))
        x = jnp.swapaxes(u_ref[:, :, sl], 0, 1).reshape(rows, LRU_BW)
        z = jnp.dot(x.astype(_BF), w_ref[n], preferred_element_type=_F32)
        th_a = jnp.tanh(z[:, :LRU_BW] + ba_ref[:, sl])
        th_x = jnp.tanh(z[:, LRU_BW:] + bx_ref[:, sl])
        a = jnp.exp2(c2 + c2 * th_a)
        a_s[n] = a
        b_s[n] = jnp.sqrt(1.0 - a * a) * ((1.0 + th_x) * (0.5 * x))

    def step(i, hs):
        t = (tc - 1 - i) if reverse else i
        idx = pl.ds(pl.multiple_of(t * SCAN_BATCH, SCAN_BATCH), SCAN_BATCH)
        new = []
        for n in range(LRU_BLOCKS):
            h = a_s[n, idx, :] * hs[n] + b_s[n, idx, :]
            h_s[n, idx, :] = h
            new.append(h)
        return tuple(new)

    hs = lax.fori_loop(0, tc, step, tuple(carry[n] for n in range(LRU_BLOCKS)), unroll=SCAN_UNROLL)
    for n in range(LRU_BLOCKS):
        carry[n] = hs[n]

    for n in range(LRU_BLOCKS):
        sl = slice(n * LRU_BW, (n + 1) * LRU_BW)
        h = jnp.swapaxes(h_s[n].reshape(tc, SCAN_BATCH, LRU_BW), 0, 1)
        if finalize:
            o_ref[:, :, sl] = ((hf_ref[:, :, sl] + h) * gate_ref[:, :, sl].astype(_F32)).astype(o_ref.dtype)
        else:
            o_ref[:, :, sl] = h


def _scan(u, w, ba, bx, lam, *, tc, reverse, h_fwd=None, gate=None):
    b, s, d = u.shape
    nt = s // tc
    finalize = h_fwd is not None
    tmap = (lambda g, c: (g, nt - 1 - c, 0)) if reverse else (lambda g, c: (g, c, 0))
    blk = pl.BlockSpec((SCAN_BATCH, tc, d), tmap)
    in_specs = [blk, _const_spec(w.shape), _const_spec((1, d)), _const_spec((1, d)), _const_spec((1, d))]
    args = [u, w, ba, bx, lam]
    if finalize:
        in_specs += [blk, blk]
        args += [h_fwd, gate]
    work = pltpu.VMEM((LRU_BLOCKS, SCAN_BATCH * tc, LRU_BW), _F32)
    return pl.pallas_call(
        functools.partial(_scan_kernel, tc=tc, reverse=reverse, finalize=finalize),
        grid=(b // SCAN_BATCH, nt),
        in_specs=in_specs,
        out_specs=blk,
        out_shape=jax.ShapeDtypeStruct((b, s, d), _BF if finalize else _F32),
        scratch_shapes=[work, work, work, pltpu.VMEM((LRU_BLOCKS, SCAN_BATCH, LRU_BW), _F32)],
        compiler_params=_cparams("parallel", "arbitrary"),
        name="scan_bwd" if reverse else "scan_fwd",
    )(*args)


def _rope_tables(pos_lo, pos_hi, n_rot, theta, s):
    half = n_rot // 2
    inv = jnp.power(jnp.float32(theta), -jnp.arange(half, dtype=_F32) * (2.0 / n_rot))

    def part(pos):
        ang = pos.astype(_F32)[:, None] * inv[None, :]
        c, sn = jnp.cos(ang), jnp.sin(ang)
        return jnp.concatenate([c, c], axis=-1), jnp.concatenate([-sn, sn], axis=-1)

    cs, sns = [], []
    for pos in (pos_lo, pos_hi):
        if pos is not None:
            c, sn = part(pos)
            cs.append(c)
            sns.append(sn)
    width = sum(c.shape[-1] for c in cs)
    cs.append(jnp.ones((s, HEAD_DIM - width), _F32))
    sns.append(jnp.zeros((s, HEAD_DIM - width), _F32))
    c = jnp.concatenate(cs, axis=-1)
    sn = jnp.concatenate(sns, axis=-1)
    return jnp.concatenate([c, c], axis=-1), jnp.concatenate([sn, sn], axis=-1)


def _even_w_in_layout(w):
    d = w.shape[0]
    aq = w[:, :A_Q_W].reshape(d, A_Q_HEADS, HEAD_DIM)[:, jnp.array(A_HEAD_ORDER), :].reshape(d, A_Q_W)
    akv = w[:, A_Q_W:A_Q_W + 2 * LANES]
    pad = jnp.zeros((d, 1024 - A_Q_W - 2 * LANES), w.dtype)
    rest = w[:, A_Q_W + 2 * LANES:]
    w_nat = jnp.concatenate([aq, akv, pad, rest[:, :3 * B_W]], axis=1).astype(_BF)
    return w_nat, rest[:, 3 * B_W:].astype(_BF)


def _even_w_out_layout(w):
    d = w.shape[1]
    wa = w[:A_Q_W].reshape(A_Q_HEADS, HEAD_DIM, d)[jnp.array(A_HEAD_ORDER)].reshape(A_Q_W, d)
    return jnp.concatenate([wa, w[A_Q_W:]], axis=0).astype(_BF)


def kernel(x, mix_norm, ffn_norm, final_norm, even_w_in, even_q_norm, even_k_norm, even_w_out, odd_w_in, odd_conv_w, odd_conv_b, odd_gate_a_w, odd_gate_a_b, odd_gate_x_w, odd_gate_x_b, odd_lambda, odd_w_out, ffn_w_gate_up, ffn_w_down):
    b, s, d = x.shape
    n = b * s
    depth = mix_norm.shape[0]
    m16 = s // PERM
    assert d == D_MODEL and s % 512 == 0 and b % SCAN_BATCH == 0

    pos = jnp.arange(s, dtype=jnp.int32)
    cos_a, sin_a = _rope_tables(pos // GRID_W, pos % GRID_W, HEAD_DIM // 2, AXIAL_THETA, s)
    cos_b, sin_b = _rope_tables(pos, None, ROPE_DIMS, ROPE_THETA, s)
    cos_p = cos_b.reshape(m16, PERM, LANES).transpose(1, 0, 2)
    sin_p = sin_b.reshape(m16, PERM, LANES).transpose(1, 0, 2)

    tm = 512
    xf = x.reshape(n, d)
    for layer in range(depth):
        j = layer // 2
        g_mix = mix_norm[layer].reshape(1, d)
        if layer % 2 == 0:
            w_nat, w_perm = _even_w_in_layout(even_w_in[j])
            nat, perm = _even_in(xf, g_mix, w_nat, w_perm, b=b, s=s, tm=tm)
            nat3 = nat.reshape(b, s, NAT_COLS)
            gq = jnp.tile(even_q_norm[j], 2).reshape(1, LANES)
            gk = jnp.tile(even_k_norm[j], 2).reshape(1, LANES)
            a_o = _attn_a(nat3, cos_a, sin_a, gq, gk, tq=128).reshape(n, A_Q_W)
            o0, l0 = _attn_b1(nat3, cos_b, sin_b)
            o1, l1 = _attn_b4(perm, cos_p, sin_p)
            o2, l2 = _attn_b16(perm, cos_p, sin_p)
            xf = _even_out(xf, a_o, o0, l0, o1, l1, o2, l2, _even_w_out_layout(even_w_out[j]), s=s, tm=tm)
        else:
            gate, u = _odd_in(xf, g_mix, odd_w_in[j].astype(_BF), tm=tm)
            uc = _conv(u.reshape(b, s, D_RNN), odd_conv_w[j], odd_conv_b[j].reshape(1, D_RNN))
            h = None
            for direction in range(2):
                w = (0.5 * jnp.concatenate([odd_gate_a_w[j, direction], odd_gate_x_w[j, direction]], axis=-1)).astype(_BF)
                h = _scan(uc, w, 0.5 * odd_gate_a_b[j, direction].reshape(1, D_RNN), 0.5 * odd_gate_x_b[j, direction].reshape(1, D_RNN),
                          odd_lambda[j, direction].reshape(1, D_RNN), tc=128, reverse=direction == 1,
                          h_fwd=h, gate=None if direction == 0 else gate.reshape(b, s, D_RNN))
            xf = _res_proj(xf, h.reshape(n, D_RNN), odd_w_out[j].astype(_BF), tm=tm)
        xf = _ffn(xf, ffn_norm[layer].reshape(1, d), ffn_w_gate_up[layer].astype(_BF), ffn_w_down[layer].astype(_BF),
                  final_norm.reshape(1, d), tm=tm, final_norm=layer == depth - 1)
    return xf.reshape(b, s, d)
```

```python
import functools

import jax
import jax.numpy as jnp
from jax import lax
from jax.experimental import pallas as pl
from jax.experimental.pallas import tpu as pltpu

D_MODEL = 1024
HEAD_DIM = 64
A_Q_HEADS = 8
A_KV_HEADS = 2
B_HEADS = 8
B_RADIUS = 64
GRID_W = 64
AXIAL_THETA = 10000.0
ROPE_THETA = 500000.0
ROPE_DIMS = HEAD_DIM // 4
D_RNN = D_MODEL
LRU_BLOCKS = 8
LRU_BW = D_RNN // LRU_BLOCKS
LRU_C = 8.0
D_FF = 2816
EPS = 1e-6
NEG_INF = -1e30
LOG2E = 1.4426950408889634
LN2 = 0.6931471805599453
Q_SCALE_LOG2 = HEAD_DIM ** -0.5 * LOG2E

LANES = 128
A_Q_W = A_Q_HEADS * HEAD_DIM
B_W = B_HEADS * HEAD_DIM
N_SLAB = B_W // LANES
A_HEAD_ORDER = (0, 4, 1, 5, 2, 6, 3, 7)
NAT_COLS = 1024 + 3 * B_W
PERM_COLS = 2 * 3 * B_W
PERM = 16
PROJ_CHUNK = 512

VMEM_LIMIT = 56 * 1024 * 1024

_BF = jnp.bfloat16
_F32 = jnp.float32


def _cparams(*sem):
    return pltpu.CompilerParams(dimension_semantics=sem, vmem_limit_bytes=VMEM_LIMIT)


def _const_spec(shape):
    return pl.BlockSpec(shape, lambda *_: (0,) * len(shape), pipeline_mode=pl.Buffered(1))


def _rms_rows(x, g):
    ms = jnp.mean(x * x, axis=-1, keepdims=True)
    return x * lax.rsqrt(ms + EPS) * g


def _sigmoid(x):
    return 0.5 * (1.0 + jnp.tanh(0.5 * x))


def _lane(shape):
    return lax.broadcasted_iota(jnp.int32, shape, len(shape) - 1)


def _to_residue_major(y):
    rows, c = y.shape
    return jnp.swapaxes(y.reshape(rows // PERM, PERM, c), 0, 1).reshape(rows, c)


def _from_residue_major(y):
    p, m, c = y.shape
    return jnp.swapaxes(y, 0, 1).reshape(p * m, c)


def _even_in_kernel(x_ref, g_ref, wn_ref, wp_ref, nat_ref, perm_ref, hn_ref, hp_ref):
    tm = x_ref.shape[0]
    y = _rms_rows(x_ref[...], g_ref[...])
    hn_ref[...] = y.astype(_BF)
    hp_ref[...] = _to_residue_major(y).astype(_BF)
    for c in range(wn_ref.shape[1] // PROJ_CHUNK):
        sl = slice(c * PROJ_CHUNK, (c + 1) * PROJ_CHUNK)
        nat_ref[:, sl] = jnp.dot(hn_ref[...], wn_ref[:, sl], preferred_element_type=_F32).astype(_BF)
    for c in range(wp_ref.shape[1] // PROJ_CHUNK):
        sl = slice(c * PROJ_CHUNK, (c + 1) * PROJ_CHUNK)
        acc = jnp.dot(hp_ref[...], wp_ref[:, sl], preferred_element_type=_F32)
        perm_ref[:, :, sl] = acc.reshape(PERM, tm // PERM, PROJ_CHUNK).astype(_BF)


def _even_in(x, g, w_nat, w_perm, *, b, s, tm):
    n, d = x.shape
    tpb = s // tm
    cn, cp = w_nat.shape[1], w_perm.shape[1]
    return pl.pallas_call(
        _even_in_kernel,
        grid=(n // tm,),
        in_specs=[pl.BlockSpec((tm, d), lambda i: (i, 0)), _const_spec((1, d)), _const_spec((d, cn)), _const_spec((d, cp))],
        out_specs=[
            pl.BlockSpec((tm, cn), lambda i: (i, 0)),
            pl.BlockSpec((None, PERM, tm // PERM, cp), lambda i: (i // tpb, 0, i % tpb, 0)),
        ],
        out_shape=[
            jax.ShapeDtypeStruct((n, cn), _BF),
            jax.ShapeDtypeStruct((b, PERM, s // PERM, cp), _BF),
        ],
        scratch_shapes=[pltpu.VMEM((tm, d), _BF), pltpu.VMEM((tm, d), _BF)],
        compiler_params=_cparams("parallel"),
        name="even_in",
    )(x, g, w_nat, w_perm)


def _gelu_tanh(x):
    return 0.5 * x * (1.0 + jnp.tanh(0.7978845608028654 * (x + 0.044715 * (x * x * x))))


def _odd_in_kernel(x_ref, g_ref, w_ref, gate_ref, u_ref, h_ref):
    h_ref[...] = _rms_rows(x_ref[...], g_ref[...]).astype(_BF)
    for c in range(D_RNN // PROJ_CHUNK):
        sl = slice(c * PROJ_CHUNK, (c + 1) * PROJ_CHUNK)
        acc = jnp.dot(h_ref[...], w_ref[:, sl], preferred_element_type=_F32)
        gate_ref[:, sl] = _gelu_tanh(acc).astype(gate_ref.dtype)
    for c in range(D_RNN // PROJ_CHUNK):
        sl = slice(c * PROJ_CHUNK, (c + 1) * PROJ_CHUNK)
        u_ref[:, sl] = jnp.dot(h_ref[...], w_ref[:, D_RNN + c * PROJ_CHUNK:D_RNN + (c + 1) * PROJ_CHUNK],
                               preferred_element_type=_F32)


def _odd_in(x, g, w, *, tm):
    n, d = x.shape
    return pl.pallas_call(
        _odd_in_kernel,
        grid=(n // tm,),
        in_specs=[pl.BlockSpec((tm, d), lambda i: (i, 0)), _const_spec((1, d)), _const_spec(w.shape)],
        out_specs=[pl.BlockSpec((tm, D_RNN), lambda i: (i, 0)), pl.BlockSpec((tm, D_RNN), lambda i: (i, 0))],
        out_shape=[jax.ShapeDtypeStruct((n, D_RNN), _BF), jax.ShapeDtypeStruct((n, D_RNN), _F32)],
        scratch_shapes=[pltpu.VMEM((tm, d), _BF)],
        compiler_params=_cparams("parallel"),
        name="odd_in",
    )(x, g, w)


def _rope_slab(x, cos, sin_signed, half):
    lane = _lane(x.shape)
    fwd = pltpu.roll(x, LANES - half, 1)
    bwd = pltpu.roll(x, half, 1)
    partner = jnp.where((lane % (2 * half)) < half, fwd, bwd)
    return x * cos + partner * sin_signed


def _head_rms_slab(x, g):
    lo = _lane(x.shape) < HEAD_DIM
    sq = x * x
    ms_lo = jnp.sum(jnp.where(lo, sq, 0.0), axis=-1, keepdims=True) * (1.0 / HEAD_DIM)
    ms_hi = jnp.sum(jnp.where(lo, 0.0, sq), axis=-1, keepdims=True) * (1.0 / HEAD_DIM)
    rs = jnp.where(lo, lax.rsqrt(ms_lo + EPS), lax.rsqrt(ms_hi + EPS))
    return x * rs * g


def _attn_a_kernel(q_ref, k_ref, v_ref, cq_ref, sq_ref, ck_ref, sk_ref, gq_ref, gk_ref, o_ref, kt_ref, *, tq):
    @pl.when(pl.program_id(1) == 0)
    def _():
        k = _head_rms_slab(k_ref[...].astype(_F32), gk_ref[...])
        k = _rope_slab(k, ck_ref[...], sk_ref[...], 16)
        kt_ref[...] = k.T.astype(_BF)

    lane = _lane((tq, LANES))
    lo = lane < HEAD_DIM
    cq = cq_ref[...]
    sq = sq_ref[...]
    gq = gq_ref[...]
    v = v_ref[...]
    for j in range(A_Q_W // LANES):
        q = q_ref[:, j * LANES:(j + 1) * LANES].astype(_F32)
        q = _rope_slab(_head_rms_slab(q, gq), cq, sq, 16) * Q_SCALE_LOG2
        outs = []
        for qm in (jnp.where(lo, q, 0.0), jnp.where(lo, 0.0, q)):
            s = jnp.dot(qm.astype(_BF), kt_ref[...], preferred_element_type=_F32)
            m = jnp.max(s, axis=-1, keepdims=True)
            p = jnp.exp2(s - m)
            l = jnp.sum(p, axis=-1, keepdims=True)
            outs.append(jnp.dot(p.astype(_BF), v, preferred_element_type=_F32) / l)
        o_ref[:, j * LANES:(j + 1) * LANES] = jnp.where(lo, outs[0], outs[1]).astype(o_ref.dtype)


def _attn_a(nat, cos, sin, gq, gk, *, tq):
    b, s, _ = nat.shape
    return pl.pallas_call(
        functools.partial(_attn_a_kernel, tq=tq),
        grid=(b, s // tq),
        in_specs=[
            pl.BlockSpec((None, tq, A_Q_W), lambda bi, i: (bi, i, 0)),
            pl.BlockSpec((None, s, LANES), lambda bi, i: (bi, 0, A_Q_W // LANES)),
            pl.BlockSpec((None, s, LANES), lambda bi, i: (bi, 0, A_Q_W // LANES + 1)),
            pl.BlockSpec((tq, LANES), lambda bi, i: (i, 0)),
            pl.BlockSpec((tq, LANES), lambda bi, i: (i, 0)),
            pl.BlockSpec((s, LANES), lambda bi, i: (0, 0)),
            pl.BlockSpec((s, LANES), lambda bi, i: (0, 0)),
            pl.BlockSpec((1, LANES), lambda bi, i: (0, 0)),
            pl.BlockSpec((1, LANES), lambda bi, i: (0, 0)),
        ],
        out_specs=pl.BlockSpec((None, tq, A_Q_W), lambda bi, i: (bi, i, 0)),
        out_shape=jax.ShapeDtypeStruct((b, s, A_Q_W), _BF),
        scratch_shapes=[pltpu.VMEM((LANES, s), _BF)],
        compiler_params=_cparams("parallel", "arbitrary"),
        name="attn_a",
    )(nat, nat, nat, cos, sin, cos, sin, gq, gk)


def _band_slab(q, kw, vw, valid):
    tq = q.shape[0]
    lo = _lane(q.shape) < HEAD_DIM
    qm = jnp.concatenate([jnp.where(lo, q, 0.0), jnp.where(lo, 0.0, q)], axis=0).astype(_BF)
    s = lax.dot_general(qm, kw, (((1,), (1,)), ((), ())), preferred_element_type=_F32)
    s = jnp.where(jnp.concatenate([valid, valid], axis=0), s, NEG_INF)
    mx = jnp.max(s, axis=-1, keepdims=True)
    pr = jnp.exp2(s - mx)
    l = jnp.sum(pr, axis=-1, keepdims=True)
    o = jnp.dot(pr.astype(_BF), vw, preferred_element_type=_F32) / l
    lse = (mx + jnp.log2(l)) * LN2
    return jnp.where(lo, o[:tq], o[tq:]), jnp.where(lo, lse[:tq], lse[tq:])


def _attn_b1_kernel(q_ref, k_ref, v_ref, c_ref, s_ref, o_ref, lse_ref, kr_ref, *, m_len, tq, win):
    half = ROPE_DIMS // 2
    for p in range(N_SLAB):
        sl = slice(p * LANES, (p + 1) * LANES)
        kr_ref[:, sl] = _rope_slab(k_ref[:, sl].astype(_F32), c_ref[...], s_ref[...], half).astype(_BF)

    row = lax.broadcasted_iota(jnp.int32, (tq, win), 0)
    col = lax.broadcasted_iota(jnp.int32, (tq, win), 1)

    def block(i, carry):
        q0 = pl.multiple_of(i * tq, tq)
        start = pl.multiple_of(jnp.clip(q0 - B_RADIUS, 0, m_len - win), B_RADIUS)
        valid = jnp.abs(row - col + (q0 - start)) <= B_RADIUS
        cq = c_ref[pl.ds(q0, tq), :]
        sq = s_ref[pl.ds(q0, tq), :]
        for p in range(N_SLAB):
            sl = slice(p * LANES, (p + 1) * LANES)
            q = _rope_slab(q_ref[pl.ds(q0, tq), sl].astype(_F32), cq, sq, half) * Q_SCALE_LOG2
            o, lse = _band_slab(q, kr_ref[pl.ds(start, win), sl], v_ref[pl.ds(start, win), sl], valid)
            o_ref[pl.ds(q0, tq), sl] = o.astype(o_ref.dtype)
            lse_ref[pl.ds(q0, tq), sl] = lse
        return carry

    lax.fori_loop(0, m_len // tq, block, 0, unroll=2)


def _attn_b1(nat, cos, sin):
    b, s, _ = nat.shape
    tq = 128
    win = min(tq + 2 * B_RADIUS, s)
    c0 = 1024 // B_W
    o, lse = pl.pallas_call(
        functools.partial(_attn_b1_kernel, m_len=s, tq=tq, win=win),
        grid=(b,),
        in_specs=[
            pl.BlockSpec((None, s, B_W), lambda bi: (bi, 0, c0)),
            pl.BlockSpec((None, s, B_W), lambda bi: (bi, 0, c0 + 1)),
            pl.BlockSpec((None, s, B_W), lambda bi: (bi, 0, c0 + 2)),
            _const_spec((s, LANES)),
            _const_spec((s, LANES)),
        ],
        out_specs=[pl.BlockSpec((None, s, B_W), lambda bi: (bi, 0, 0))] * 2,
        out_shape=[jax.ShapeDtypeStruct((b, s, B_W), _BF), jax.ShapeDtypeStruct((b, s, B_W), _F32)],
        scratch_shapes=[pltpu.VMEM((s, B_W), _BF)],
        compiler_params=_cparams("parallel"),
        name="attn_b1",
    )(nat, nat, nat, cos, sin)
    return o.reshape(b * s, B_W), lse.reshape(b * s, B_W)


def _attn_b4_kernel(q_ref, k_ref, v_ref, c_ref, s_ref, o_ref, lse_ref, kr_ref, *, m16):
    half = ROPE_DIMS // 2
    seg = 4
    tqs = 32
    wks = min(tqs + 2 * (B_RADIUS // seg), m16)
    for j in range(seg):
        for p in range(N_SLAB):
            sl = slice(p * LANES, (p + 1) * LANES)
            kr_ref[j, :, sl] = _rope_slab(k_ref[j, :, sl].astype(_F32), c_ref[j], s_ref[j], half).astype(_BF)

    tq, win = seg * tqs, seg * wks
    row = lax.broadcasted_iota(jnp.int32, (tq, win), 0)
    col = lax.broadcasted_iota(jnp.int32, (tq, win), 1)
    for i in range(m16 // tqs):
        a_q = i * tqs
        a_k = min(max(a_q - B_RADIUS // seg, 0), m16 - wks)
        qpos = seg * (a_q + (row & (tqs - 1))) + (row >> (tqs.bit_length() - 1))
        kpos = seg * (a_k + (col & (wks - 1))) + (col >> (wks.bit_length() - 1))
        valid = jnp.abs(qpos - kpos) <= B_RADIUS
        cq = jnp.concatenate([c_ref[j, a_q:a_q + tqs, :] for j in range(seg)], axis=0)
        sq = jnp.concatenate([s_ref[j, a_q:a_q + tqs, :] for j in range(seg)], axis=0)
        for p in range(N_SLAB):
            sl = slice(p * LANES, (p + 1) * LANES)
            q = jnp.concatenate([q_ref[j, a_q:a_q + tqs, sl] for j in range(seg)], axis=0).astype(_F32)
            q = _rope_slab(q, cq, sq, half) * Q_SCALE_LOG2
            kw = jnp.concatenate([kr_ref[j, a_k:a_k + wks, sl] for j in range(seg)], axis=0)
            vw = jnp.concatenate([v_ref[j, a_k:a_k + wks, sl] for j in range(seg)], axis=0)
            o, lse = _band_slab(q, kw, vw, valid)
            for j in range(seg):
                o_ref[j, a_q:a_q + tqs, sl] = o[j * tqs:(j + 1) * tqs].astype(o_ref.dtype)
                lse_ref[j, a_q:a_q + tqs, sl] = lse[j * tqs:(j + 1) * tqs]


def _attn_b4(perm, cos_p, sin_p):
    b, _, m16, cols = perm.shape
    pv = perm.reshape(b, 4, 4, m16, cols)
    cv = cos_p.reshape(4, 4, m16, LANES)
    sv = sin_p.reshape(4, 4, m16, LANES)
    blk = lambda cb: pl.BlockSpec((None, 4, None, m16, B_W), lambda bi, r: (bi, 0, r, 0, cb))
    tab = pl.BlockSpec((4, None, m16, LANES), lambda bi, r: (0, r, 0, 0))
    o, lse = pl.pallas_call(
        functools.partial(_attn_b4_kernel, m16=m16),
        grid=(b, 4),
        in_specs=[blk(0), blk(1), blk(2), tab, tab],
        out_specs=[blk(0), blk(0)],
        out_shape=[jax.ShapeDtypeStruct((b, 4, 4, m16, B_W), _BF), jax.ShapeDtypeStruct((b, 4, 4, m16, B_W), _F32)],
        scratch_shapes=[pltpu.VMEM((4, m16, B_W), _BF)],
        compiler_params=_cparams("parallel", "parallel"),
        name="attn_b4",
    )(pv, pv, pv, cv, sv)
    return o.reshape(b, PERM, m16, B_W), lse.reshape(b, PERM, m16, B_W)


def _attn_b16_kernel(q_ref, k_ref, v_ref, c_ref, s_ref, o_ref, lse_ref, *, m16, tq, win):
    half = ROPE_DIMS // 2
    row = lax.broadcasted_iota(jnp.int32, (tq, win), 0)
    col = lax.broadcasted_iota(jnp.int32, (tq, win), 1)

    def residue(r, carry):
        for i in range(m16 // tq):
            q0 = i * tq
            start = min(max(q0 - B_RADIUS, 0), m16 - win)
            valid = jnp.abs(row - col + (q0 - start)) <= B_RADIUS
            for p in range(N_SLAB):
                sl = slice(p * LANES, (p + 1) * LANES)
                q = _rope_slab(q_ref[r, q0:q0 + tq, sl].astype(_F32), c_ref[r, q0:q0 + tq, :], s_ref[r, q0:q0 + tq, :], half)
                kw = _rope_slab(k_ref[r, start:start + win, sl].astype(_F32), c_ref[r, start:start + win, :],
                                s_ref[r, start:start + win, :], half).astype(_BF)
                o, lse = _band_slab(q * Q_SCALE_LOG2, kw, v_ref[r, start:start + win, sl], valid)
                o_ref[r, q0:q0 + tq, sl] = o.astype(o_ref.dtype)
                lse_ref[r, q0:q0 + tq, sl] = lse
        return carry

    lax.fori_loop(0, PERM, residue, 0, unroll=2)


def _attn_b16(perm, cos_p, sin_p):
    b, _, m16, cols = perm.shape
    tq = min(128, m16)
    win = min(tq + 2 * B_RADIUS, m16)
    c0 = 3
    blk = lambda cb: pl.BlockSpec((None, PERM, m16, B_W), lambda bi: (bi, 0, 0, cb))
    return pl.pallas_call(
        functools.partial(_attn_b16_kernel, m16=m16, tq=tq, win=win),
        grid=(b,),
        in_specs=[blk(c0), blk(c0 + 1), blk(c0 + 2), _const_spec((PERM, m16, LANES)), _const_spec((PERM, m16, LANES))],
        out_specs=[blk(0), blk(0)],
        out_shape=[jax.ShapeDtypeStruct((b, PERM, m16, B_W), _BF), jax.ShapeDtypeStruct((b, PERM, m16, B_W), _F32)],
        compiler_params=_cparams("parallel"),
        name="attn_b16",
    )(perm, perm, perm, cos_p, sin_p)


def _even_out_kernel(x_ref, a_ref, o0_ref, l0_ref, o1_ref, l1_ref, o2_ref, l2_ref, w_ref, y_ref):
    l0 = l0_ref[...]
    l1 = _from_residue_major(l1_ref[...])
    l2 = _from_residue_major(l2_ref[...])
    o0 = o0_ref[...].astype(_F32)
    o1 = _from_residue_major(o1_ref[...].astype(_F32))
    o2 = _from_residue_major(o2_ref[...].astype(_F32))
    mx = jnp.maximum(jnp.maximum(l0, l1), l2)
    e0, e1, e2 = jnp.exp(l0 - mx), jnp.exp(l1 - mx), jnp.exp(l2 - mx)
    bo = (e0 * o0 + e1 * o1 + e2 * o2) / (e0 + e1 + e2)
    y = jnp.dot(a_ref[...], w_ref[:A_Q_W, :], preferred_element_type=_F32)
    y = y + jnp.dot(bo.astype(_BF), w_ref[A_Q_W:, :], preferred_element_type=_F32)
    y_ref[...] = x_ref[...] + y


def _even_out(x, a_o, o0, l0, o1, l1, o2, l2, w, *, s, tm):
    n, d = x.shape
    tpb = s // tm
    row = lambda width: pl.BlockSpec((tm, width), lambda i: (i, 0))
    prm = pl.BlockSpec((None, PERM, tm // PERM, B_W), lambda i: (i // tpb, 0, i % tpb, 0))
    return pl.pallas_call(
        _even_out_kernel,
        grid=(n // tm,),
        in_specs=[row(d), row(A_Q_W), row(B_W), row(B_W), prm, prm, prm, prm, _const_spec(w.shape)],
        out_specs=row(d),
        out_shape=jax.ShapeDtypeStruct((n, d), _F32),
        compiler_params=_cparams("parallel"),
        name="even_out",
    )(x, a_o, o0, l0, o1, l1, o2, l2, w)


def _ffn_kernel(*refs, chunk, final_norm, mixer_proj):
    if mixer_proj:
        x_ref, a_ref, wo_ref, g_ref, wg_ref, wu_ref, wd_ref, gf_ref, y_ref, h_ref, act_ref = refs
        x = x_ref[...] + jnp.dot(a_ref[...], wo_ref[...], preferred_element_type=_F32)
    else:
        x_ref, g_ref, wg_ref, wu_ref, wd_ref, gf_ref, y_ref, h_ref, act_ref = refs
        x = x_ref[...]
    h_ref[...] = _rms_rows(x, g_ref[...]).astype(_BF)
    for c in range(D_FF // chunk):
        sl = slice(c * chunk, (c + 1) * chunk)
        gate = jnp.dot(h_ref[...], wg_ref[:, sl], preferred_element_type=_F32)
        up = jnp.dot(h_ref[...], wu_ref[:, sl], preferred_element_type=_F32)
        act_ref[:, sl] = (gate * _sigmoid(gate) * up).astype(_BF)
    y = x + jnp.dot(act_ref[...], wd_ref[...], preferred_element_type=_F32)
    if final_norm:
        y = _rms_rows(y, gf_ref[...])
    y_ref[...] = y


def _ffn(x, g, w_gate_up, w_down, g_final, *, tm, final_norm, mixer_out=None, w_mixer=None):
    n, d = x.shape
    const = lambda shape, idx: pl.BlockSpec(shape, lambda i: idx, pipeline_mode=pl.Buffered(1))
    mixer_proj = mixer_out is not None
    row = lambda width: pl.BlockSpec((tm, width), lambda i: (i, 0))
    in_specs = [row(d)]
    args = [x]
    if mixer_proj:
        in_specs += [row(mixer_out.shape[1]), const(w_mixer.shape, (0, 0))]
        args += [mixer_out, w_mixer]
    in_specs += [
        const((1, d), (0, 0)),
        const((d, D_FF), (0, 0)),
        const((d, D_FF), (0, 1)),
        const((D_FF, d), (0, 0)),
        const((1, d), (0, 0)),
    ]
    args += [g, w_gate_up, w_gate_up, w_down, g_final]
    return pl.pallas_call(
        functools.partial(_ffn_kernel, chunk=256, final_norm=final_norm, mixer_proj=mixer_proj),
        grid=(n // tm,),
        in_specs=in_specs,
        out_specs=row(d),
        out_shape=jax.ShapeDtypeStruct((n, d), _F32),
        scratch_shapes=[pltpu.VMEM((tm, d), _BF), pltpu.VMEM((tm, D_FF), _BF)],
        compiler_params=_cparams("parallel"),
        name="ffn_mix" if mixer_proj else "ffn",
    )(*args)


def _conv_kernel(u_ref, w_ref, b_ref, o_ref):
    s = u_ref.shape[0]
    t = lax.broadcasted_iota(jnp.int32, (s, LANES), 0)
    for c in range(D_RNN // LANES):
        sl = slice(c * LANES, (c + 1) * LANES)
        u = u_ref[:, sl]
        um1 = jnp.where(t >= 1, pltpu.roll(u, 1, 0), 0.0)
        up1 = jnp.where(t < s - 1, pltpu.roll(u, s - 1, 0), 0.0)
        up2 = jnp.where(t < s - 2, pltpu.roll(u, s - 2, 0), 0.0)
        w = w_ref[:, sl]
        o_ref[:, sl] = um1 * w[0:1] + u * w[1:2] + up1 * w[2:3] + up2 * w[3:4] + b_ref[:, sl]


def _conv(u, w, bias):
    b, s, d = u.shape
    return pl.pallas_call(
        _conv_kernel,
        grid=(b,),
        in_specs=[pl.BlockSpec((None, s, d), lambda i: (i, 0, 0)), _const_spec(w.shape), _const_spec((1, d))],
        out_specs=pl.BlockSpec((None, s, d), lambda i: (i, 0, 0)),
        out_shape=jax.ShapeDtypeStruct((b, s, d), _F32),
        compiler_params=_cparams("parallel"),
        name="conv",
    )(u, w, bias)


SCAN_BATCH = 8
SCAN_UNROLL = 4


def _scan_kernel(*refs, tc, reverse, finalize):
    if finalize:
        u_ref, w_ref, ba_ref, bx_ref, lam_ref, hf_ref, gate_ref, o_ref, a_s, b_s, h_s, carry = refs
    else:
        u_ref, w_ref, ba_ref, bx_ref, lam_ref, o_ref, a_s, b_s, h_s, carry = refs
    rows = SCAN_BATCH * tc

    @pl.when(pl.program_id(1) == 0)
    def _():
        carry[...] = jnp.zeros_like(carry)

    for n in range(LRU_BLOCKS):
        sl = slice(n * LRU_BW, (n + 1) * LRU_BW)
        lam = lam_ref[:, sl]
        c2 = (-4.0 * LOG2E) * (jnp.maximum(-lam, 0.0) + jnp.log1p(jnp.exp(-jnp.abs(lam))))
        x = jnp.swapaxes(u_ref[:, :, sl], 0, 1).reshape(rows, LRU_BW)
        z = jnp.dot(x.astype(_BF), w_ref[n], preferred_element_type=_F32)
        th_a = jnp.tanh(z[:, :LRU_BW] + ba_ref[:, sl])
        th_x = jnp.tanh(z[:, LRU_BW:] + bx_ref[:, sl])
        a = jnp.exp2(c2 + c2 * th_a)
        a_s[n] = a
        v = 1.0 - a * a
        root = jnp.where(v > 0.0, v * lax.rsqrt(v), 0.0)
        b_s[n] = root * ((1.0 + th_x) * (0.5 * x))

    def step(i, hs):
        t = (tc - 1 - i) if reverse else i
        idx = pl.ds(pl.multiple_of(t * SCAN_BATCH, SCAN_BATCH), SCAN_BATCH)
        new = []
        for n in range(LRU_BLOCKS):
            h = a_s[n, idx, :] * hs[n] + b_s[n, idx, :]
            h_s[n, idx, :] = h
            new.append(h)
        return tuple(new)

    hs = lax.fori_loop(0, tc, step, tuple(carry[n] for n in range(LRU_BLOCKS)), unroll=SCAN_UNROLL)
    for n in range(LRU_BLOCKS):
        carry[n] = hs[n]

    for n in range(LRU_BLOCKS):
        sl = slice(n * LRU_BW, (n + 1) * LRU_BW)
        h = jnp.swapaxes(h_s[n].reshape(tc, SCAN_BATCH, LRU_BW), 0, 1)
        if finalize:
            o_ref[:, :, sl] = ((hf_ref[:, :, sl] + h) * gate_ref[:, :, sl].astype(_F32)).astype(o_ref.dtype)
        else:
            o_ref[:, :, sl] = h


def _scan(u, w, ba, bx, lam, *, tc, reverse, h_fwd=None, gate=None):
    b, s, d = u.shape
    nt = s // tc
    finalize = h_fwd is not None
    tmap = (lambda g, c: (g, nt - 1 - c, 0)) if reverse else (lambda g, c: (g, c, 0))
    blk = pl.BlockSpec((SCAN_BATCH, tc, d), tmap)
    in_specs = [blk, _const_spec(w.shape), _const_spec((1, d)), _const_spec((1, d)), _const_spec((1, d))]
    args = [u, w, ba, bx, lam]
    if finalize:
        in_specs += [blk, blk]
        args += [h_fwd, gate]
    work = pltpu.VMEM((LRU_BLOCKS, SCAN_BATCH * tc, LRU_BW), _F32)
    return pl.pallas_call(
        functools.partial(_scan_kernel, tc=tc, reverse=reverse, finalize=finalize),
        grid=(b // SCAN_BATCH, nt),
        in_specs=in_specs,
        out_specs=blk,
        out_shape=jax.ShapeDtypeStruct((b, s, d), _BF if finalize else _F32),
        scratch_shapes=[work, work, work, pltpu.VMEM((LRU_BLOCKS, SCAN_BATCH, LRU_BW), _F32)],
        compiler_params=_cparams("parallel", "arbitrary"),
        name="scan_bwd" if reverse else "scan_fwd",
    )(*args)


def _rope_tables(pos_lo, pos_hi, n_rot, theta, s):
    half = n_rot // 2
    inv = jnp.power(jnp.float32(theta), -jnp.arange(half, dtype=_F32) * (2.0 / n_rot))

    def part(pos):
        ang = pos.astype(_F32)[:, None] * inv[None, :]
        c, sn = jnp.cos(ang), jnp.sin(ang)
        return jnp.concatenate([c, c], axis=-1), jnp.concatenate([-sn, sn], axis=-1)

    cs, sns = [], []
    for pos in (pos_lo, pos_hi):
        if pos is not None:
            c, sn = part(pos)
            cs.append(c)
            sns.append(sn)
    width = sum(c.shape[-1] for c in cs)
    cs.append(jnp.ones((s, HEAD_DIM - width), _F32))
    sns.append(jnp.zeros((s, HEAD_DIM - width), _F32))
    c = jnp.concatenate(cs, axis=-1)
    sn = jnp.concatenate(sns, axis=-1)
    return jnp.concatenate([c, c], axis=-1), jnp.concatenate([sn, sn], axis=-1)


def _even_w_in_layout(w):
    d = w.shape[0]
    aq = w[:, :A_Q_W].reshape(d, A_Q_HEADS, HEAD_DIM)[:, jnp.array(A_HEAD_ORDER), :].reshape(d, A_Q_W)
    akv = w[:, A_Q_W:A_Q_W + 2 * LANES]
    pad = jnp.zeros((d, 1024 - A_Q_W - 2 * LANES), w.dtype)
    rest = w[:, A_Q_W + 2 * LANES:]
    w_nat = jnp.concatenate([aq, akv, pad, rest[:, :3 * B_W]], axis=1).astype(_BF)
    return w_nat, rest[:, 3 * B_W:].astype(_BF)


def _even_w_out_layout(w):
    d = w.shape[1]
    wa = w[:A_Q_W].reshape(A_Q_HEADS, HEAD_DIM, d)[jnp.array(A_HEAD_ORDER)].reshape(A_Q_W, d)
    return jnp.concatenate([wa, w[A_Q_W:]], axis=0).astype(_BF)


def kernel(x, mix_norm, ffn_norm, final_norm, even_w_in, even_q_norm, even_k_norm, even_w_out, odd_w_in, odd_conv_w, odd_conv_b, odd_gate_a_w, odd_gate_a_b, odd_gate_x_w, odd_gate_x_b, odd_lambda, odd_w_out, ffn_w_gate_up, ffn_w_down):
    b, s, d = x.shape
    n = b * s
    depth = mix_norm.shape[0]
    m16 = s // PERM
    assert d == D_MODEL and s % 512 == 0 and b % SCAN_BATCH == 0

    pos = jnp.arange(s, dtype=jnp.int32)
    cos_a, sin_a = _rope_tables(pos // GRID_W, pos % GRID_W, HEAD_DIM // 2, AXIAL_THETA, s)
    cos_b, sin_b = _rope_tables(pos, None, ROPE_DIMS, ROPE_THETA, s)
    cos_p = cos_b.reshape(m16, PERM, LANES).transpose(1, 0, 2)
    sin_p = sin_b.reshape(m16, PERM, LANES).transpose(1, 0, 2)

    tm = 512
    xf = x.reshape(n, d)
    for layer in range(depth):
        j = layer // 2
        g_mix = mix_norm[layer].reshape(1, d)
        if layer % 2 == 0:
            w_nat, w_perm = _even_w_in_layout(even_w_in[j])
            nat, perm = _even_in(xf, g_mix, w_nat, w_perm, b=b, s=s, tm=tm)
            nat3 = nat.reshape(b, s, NAT_COLS)
            gq = jnp.tile(even_q_norm[j], 2).reshape(1, LANES)
            gk = jnp.tile(even_k_norm[j], 2).reshape(1, LANES)
            a_o = _attn_a(nat3, cos_a, sin_a, gq, gk, tq=256).reshape(n, A_Q_W)
            o0, l0 = _attn_b1(nat3, cos_b, sin_b)
            o1, l1 = _attn_b4(perm, cos_p, sin_p)
            o2, l2 = _attn_b16(perm, cos_p, sin_p)
            xf = _even_out(xf, a_o, o0, l0, o1, l1, o2, l2, _even_w_out_layout(even_w_out[j]), s=s, tm=tm)
            mix = {}
        else:
            gate, u = _odd_in(xf, g_mix, odd_w_in[j].astype(_BF), tm=tm)
            uc = _conv(u.reshape(b, s, D_RNN), odd_conv_w[j], odd_conv_b[j].reshape(1, D_RNN))
            h = None
            for direction in range(2):
                w = (0.5 * jnp.concatenate([odd_gate_a_w[j, direction], odd_gate_x_w[j, direction]], axis=-1)).astype(_BF)
                h = _scan(uc, w, 0.5 * odd_gate_a_b[j, direction].reshape(1, D_RNN), 0.5 * odd_gate_x_b[j, direction].reshape(1, D_RNN),
                          odd_lambda[j, direction].reshape(1, D_RNN), tc=128, reverse=direction == 1,
                          h_fwd=h, gate=None if direction == 0 else gate.reshape(b, s, D_RNN))
            mix = dict(mixer_out=h.reshape(n, D_RNN), w_mixer=odd_w_out[j].astype(_BF))
        xf = _ffn(xf, ffn_norm[layer].reshape(1, d), ffn_w_gate_up[layer].astype(_BF), ffn_w_down[layer].astype(_BF),
                  final_norm.reshape(1, d), tm=tm, final_norm=layer == depth - 1, **mix)
    return xf.reshape(b, s, d)
```

```python
import functools

import jax
import jax.numpy as jnp
from jax import lax
from jax.experimental import pallas as pl
from jax.experimental.pallas import tpu as pltpu

D_MODEL = 1024
HEAD_DIM = 64
A_Q_HEADS = 8
A_KV_HEADS = 2
B_HEADS = 8
B_RADIUS = 64
GRID_W = 64
AXIAL_THETA = 10000.0
ROPE_THETA = 500000.0
ROPE_DIMS = HEAD_DIM // 4
D_RNN = D_MODEL
LRU_BLOCKS = 8
LRU_BW = D_RNN // LRU_BLOCKS
LRU_C = 8.0
D_FF = 2816
EPS = 1e-6
NEG_INF = -1e30
LOG2E = 1.4426950408889634
LN2 = 0.6931471805599453
Q_SCALE_LOG2 = HEAD_DIM ** -0.5 * LOG2E

LANES = 128
A_Q_W = A_Q_HEADS * HEAD_DIM
B_W = B_HEADS * HEAD_DIM
N_SLAB = B_W // LANES
A_CHAIN_ROWS = 256
A_SAFE_SCORE = 50.0
A_HEAD_ORDER = (0, 4, 1, 5, 2, 6, 3, 7)
NAT_COLS = 1024 + 3 * B_W
PERM_COLS = 2 * 3 * B_W
PERM = 16
PROJ_CHUNK = 512

VMEM_LIMIT = 56 * 1024 * 1024

_BF = jnp.bfloat16
_F32 = jnp.float32


def _cparams(*sem):
    return pltpu.CompilerParams(dimension_semantics=sem, vmem_limit_bytes=VMEM_LIMIT)


def _const_spec(shape):
    return pl.BlockSpec(shape, lambda *_: (0,) * len(shape), pipeline_mode=pl.Buffered(1))


def _rms_rows(x, g):
    ms = jnp.mean(x * x, axis=-1, keepdims=True)
    return x * lax.rsqrt(ms + EPS) * g


def _sigmoid(x):
    return 0.5 * (1.0 + jnp.tanh(0.5 * x))


def _lane(shape):
    return lax.broadcasted_iota(jnp.int32, shape, len(shape) - 1)


def _to_residue_major(y):
    rows, c = y.shape
    return jnp.swapaxes(y.reshape(rows // PERM, PERM, c), 0, 1).reshape(rows, c)


def _from_residue_major(y):
    p, m, c = y.shape
    return jnp.swapaxes(y, 0, 1).reshape(p * m, c)


def _even_in_kernel(x_ref, g_ref, wn_ref, wp_ref, nat_ref, perm_ref, hn_ref, hp_ref):
    tm = x_ref.shape[0]
    y = _rms_rows(x_ref[...], g_ref[...])
    hn_ref[...] = y.astype(_BF)
    hp_ref[...] = _to_residue_major(y).astype(_BF)
    for c in range(wn_ref.shape[1] // PROJ_CHUNK):
        sl = slice(c * PROJ_CHUNK, (c + 1) * PROJ_CHUNK)
        nat_ref[:, sl] = jnp.dot(hn_ref[...], wn_ref[:, sl], preferred_element_type=_F32).astype(_BF)
    for c in range(wp_ref.shape[1] // PROJ_CHUNK):
        sl = slice(c * PROJ_CHUNK, (c + 1) * PROJ_CHUNK)
        acc = jnp.dot(hp_ref[...], wp_ref[:, sl], preferred_element_type=_F32)
        perm_ref[:, :, sl] = acc.reshape(PERM, tm // PERM, PROJ_CHUNK).astype(_BF)


def _even_in(x, g, w_nat, w_perm, *, b, s, tm):
    n, d = x.shape
    tpb = s // tm
    cn, cp = w_nat.shape[1], w_perm.shape[1]
    return pl.pallas_call(
        _even_in_kernel,
        grid=(n // tm,),
        in_specs=[pl.BlockSpec((tm, d), lambda i: (i, 0)), _const_spec((1, d)), _const_spec((d, cn)), _const_spec((d, cp))],
        out_specs=[
            pl.BlockSpec((tm, cn), lambda i: (i, 0)),
            pl.BlockSpec((None, PERM, tm // PERM, cp), lambda i: (i // tpb, 0, i % tpb, 0)),
        ],
        out_shape=[
            jax.ShapeDtypeStruct((n, cn), _BF),
            jax.ShapeDtypeStruct((b, PERM, s // PERM, cp), _BF),
        ],
        scratch_shapes=[pltpu.VMEM((tm, d), _BF), pltpu.VMEM((tm, d), _BF)],
        compiler_params=_cparams("parallel"),
        name="even_in",
    )(x, g, w_nat, w_perm)


def _gelu_tanh(x):
    return 0.5 * x * (1.0 + jnp.tanh(0.7978845608028654 * (x + 0.044715 * (x * x * x))))


def _odd_in_kernel(x_ref, g_ref, w_ref, gate_ref, u_ref, h_ref):
    h_ref[...] = _rms_rows(x_ref[...], g_ref[...]).astype(_BF)
    for c in range(D_RNN // PROJ_CHUNK):
        sl = slice(c * PROJ_CHUNK, (c + 1) * PROJ_CHUNK)
        acc = jnp.dot(h_ref[...], w_ref[:, sl], preferred_element_type=_F32)
        gate_ref[:, sl] = _gelu_tanh(acc).astype(gate_ref.dtype)
    for c in range(D_RNN // PROJ_CHUNK):
        sl = slice(c * PROJ_CHUNK, (c + 1) * PROJ_CHUNK)
        u_ref[:, sl] = jnp.dot(h_ref[...], w_ref[:, D_RNN + c * PROJ_CHUNK:D_RNN + (c + 1) * PROJ_CHUNK],
                               preferred_element_type=_F32)


def _odd_in(x, g, w, *, tm):
    n, d = x.shape
    return pl.pallas_call(
        _odd_in_kernel,
        grid=(n // tm,),
        in_specs=[pl.BlockSpec((tm, d), lambda i: (i, 0)), _const_spec((1, d)), _const_spec(w.shape)],
        out_specs=[pl.BlockSpec((tm, D_RNN), lambda i: (i, 0)), pl.BlockSpec((tm, D_RNN), lambda i: (i, 0))],
        out_shape=[jax.ShapeDtypeStruct((n, D_RNN), _BF), jax.ShapeDtypeStruct((n, D_RNN), _F32)],
        scratch_shapes=[pltpu.VMEM((tm, d), _BF)],
        compiler_params=_cparams("parallel"),
        name="odd_in",
    )(x, g, w)


def _rope_slab(x, cos, sin_signed, half):
    lane = _lane(x.shape)
    fwd = pltpu.roll(x, LANES - half, 1)
    bwd = pltpu.roll(x, half, 1)
    partner = jnp.where((lane % (2 * half)) < half, fwd, bwd)
    return x * cos + partner * sin_signed


def _head_rms_slab(x, g):
    lo = _lane(x.shape) < HEAD_DIM
    sq = x * x
    ms_lo = jnp.sum(jnp.where(lo, sq, 0.0), axis=-1, keepdims=True) * (1.0 / HEAD_DIM)
    ms_hi = jnp.sum(jnp.where(lo, 0.0, sq), axis=-1, keepdims=True) * (1.0 / HEAD_DIM)
    rs = jnp.where(lo, lax.rsqrt(ms_lo + EPS), lax.rsqrt(ms_hi + EPS))
    return x * rs * g


def _attn_a_kernel(q_ref, k_ref, v_ref, cq_ref, sq_ref, ck_ref, sk_ref, gq_ref, gk_ref, o_ref, kt_ref, *, tq):
    @pl.when(pl.program_id(1) == 0)
    def _():
        k = _head_rms_slab(k_ref[...].astype(_F32), gk_ref[...])
        k = _rope_slab(k, ck_ref[...], sk_ref[...], 16)
        kt_ref[...] = k.T.astype(_BF)

    lo = _lane((A_CHAIN_ROWS, LANES)) < HEAD_DIM
    gq = gq_ref[...]
    gk = gk_ref[...]
    bound_sq = (HEAD_DIM * Q_SCALE_LOG2) ** 2 * jnp.max(gq * gq) * jnp.max(gk * gk)
    bounded = bound_sq <= A_SAFE_SCORE * A_SAFE_SCORE

    def chains(subtract_max):
        v = v_ref[...]
        for r0 in range(0, tq, A_CHAIN_ROWS):
            rows = slice(r0, r0 + A_CHAIN_ROWS)
            cq = cq_ref[rows, :]
            sq = sq_ref[rows, :]
            for j in range(A_Q_W // LANES):
                cols = slice(j * LANES, (j + 1) * LANES)
                q = _rope_slab(_head_rms_slab(q_ref[rows, cols].astype(_F32), gq), cq, sq, 16) * Q_SCALE_LOG2
                outs = []
                for qm in (jnp.where(lo, q, 0.0), jnp.where(lo, 0.0, q)):
                    s = jnp.dot(qm.astype(_BF), kt_ref[...], preferred_element_type=_F32)
                    if subtract_max:
                        s = s - jnp.max(s, axis=-1, keepdims=True)
                    p = jnp.exp2(s)
                    l = jnp.sum(p, axis=-1, keepdims=True)
                    outs.append(jnp.dot(p.astype(_BF), v, preferred_element_type=_F32) / l)
                o_ref[rows, cols] = jnp.where(lo, outs[0], outs[1]).astype(o_ref.dtype)

    @pl.when(bounded)
    def _():
        chains(False)

    @pl.when(jnp.logical_not(bounded))
    def _():
        chains(True)


def _attn_a(nat, cos, sin, gq, gk, *, tq):
    b, s, _ = nat.shape
    return pl.pallas_call(
        functools.partial(_attn_a_kernel, tq=tq),
        grid=(b, s // tq),
        in_specs=[
            pl.BlockSpec((None, tq, A_Q_W), lambda bi, i: (bi, i, 0)),
            pl.BlockSpec((None, s, LANES), lambda bi, i: (bi, 0, A_Q_W // LANES)),
            pl.BlockSpec((None, s, LANES), lambda bi, i: (bi, 0, A_Q_W // LANES + 1)),
            pl.BlockSpec((tq, LANES), lambda bi, i: (i, 0)),
            pl.BlockSpec((tq, LANES), lambda bi, i: (i, 0)),
            pl.BlockSpec((s, LANES), lambda bi, i: (0, 0)),
            pl.BlockSpec((s, LANES), lambda bi, i: (0, 0)),
            pl.BlockSpec((1, LANES), lambda bi, i: (0, 0)),
            pl.BlockSpec((1, LANES), lambda bi, i: (0, 0)),
        ],
        out_specs=pl.BlockSpec((None, tq, A_Q_W), lambda bi, i: (bi, i, 0)),
        out_shape=jax.ShapeDtypeStruct((b, s, A_Q_W), _BF),
        scratch_shapes=[pltpu.VMEM((LANES, s), _BF)],
        compiler_params=_cparams("parallel", "arbitrary"),
        name="attn_a",
    )(nat, nat, nat, cos, sin, cos, sin, gq, gk)


def _band_slab(q, kw, vw, valid):
    tq = q.shape[0]
    lo = _lane(q.shape) < HEAD_DIM
    qm = jnp.concatenate([jnp.where(lo, q, 0.0), jnp.where(lo, 0.0, q)], axis=0).astype(_BF)
    s = lax.dot_general(qm, kw, (((1,), (1,)), ((), ())), preferred_element_type=_F32)
    s = jnp.where(jnp.concatenate([valid, valid], axis=0), s, NEG_INF)
    mx = jnp.max(s, axis=-1, keepdims=True)
    pr = jnp.exp2(s - mx)
    l = jnp.sum(pr, axis=-1, keepdims=True)
    o = jnp.dot(pr.astype(_BF), vw, preferred_element_type=_F32) / l
    lse = (mx + jnp.log2(l)) * LN2
    return jnp.where(lo, o[:tq], o[tq:]), jnp.where(lo, lse[:tq], lse[tq:])


def _attn_b1_kernel(q_ref, k_ref, v_ref, c_ref, s_ref, o_ref, lse_ref, kr_ref, *, m_len, tq, win):
    half = ROPE_DIMS // 2
    for p in range(N_SLAB):
        sl = slice(p * LANES, (p + 1) * LANES)
        kr_ref[:, sl] = _rope_slab(k_ref[:, sl].astype(_F32), c_ref[...], s_ref[...], half).astype(_BF)

    row = lax.broadcasted_iota(jnp.int32, (tq, win), 0)
    col = lax.broadcasted_iota(jnp.int32, (tq, win), 1)

    def block(i, carry):
        q0 = pl.multiple_of(i * tq, tq)
        start = pl.multiple_of(jnp.clip(q0 - B_RADIUS, 0, m_len - win), B_RADIUS)
        valid = jnp.abs(row - col + (q0 - start)) <= B_RADIUS
        cq = c_ref[pl.ds(q0, tq), :]
        sq = s_ref[pl.ds(q0, tq), :]
        for p in range(N_SLAB):
            sl = slice(p * LANES, (p + 1) * LANES)
            q = _rope_slab(q_ref[pl.ds(q0, tq), sl].astype(_F32), cq, sq, half) * Q_SCALE_LOG2
            o, lse = _band_slab(q, kr_ref[pl.ds(start, win), sl], v_ref[pl.ds(start, win), sl], valid)
            o_ref[pl.ds(q0, tq), sl] = o.astype(o_ref.dtype)
            lse_ref[pl.ds(q0, tq), sl] = lse
        return carry

    lax.fori_loop(0, m_len // tq, block, 0, unroll=4)


def _attn_b1(nat, cos, sin):
    b, s, _ = nat.shape
    tq = 128
    win = min(tq + 2 * B_RADIUS, s)
    c0 = 1024 // B_W
    o, lse = pl.pallas_call(
        functools.partial(_attn_b1_kernel, m_len=s, tq=tq, win=win),
        grid=(b,),
        in_specs=[
            pl.BlockSpec((None, s, B_W), lambda bi: (bi, 0, c0)),
            pl.BlockSpec((None, s, B_W), lambda bi: (bi, 0, c0 + 1)),
            pl.BlockSpec((None, s, B_W), lambda bi: (bi, 0, c0 + 2)),
            _const_spec((s, LANES)),
            _const_spec((s, LANES)),
        ],
        out_specs=[pl.BlockSpec((None, s, B_W), lambda bi: (bi, 0, 0))] * 2,
        out_shape=[jax.ShapeDtypeStruct((b, s, B_W), _BF), jax.ShapeDtypeStruct((b, s, B_W), _F32)],
        scratch_shapes=[pltpu.VMEM((s, B_W), _BF)],
        compiler_params=_cparams("parallel"),
        name="attn_b1",
    )(nat, nat, nat, cos, sin)
    return o.reshape(b * s, B_W), lse.reshape(b * s, B_W)


def _attn_b4_kernel(q_ref, k_ref, v_ref, c_ref, s_ref, o_ref, lse_ref, kr_ref, *, m16):
    half = ROPE_DIMS // 2
    seg = 4
    tqs = 32
    wks = min(tqs + 2 * (B_RADIUS // seg), m16)
    for j in range(seg):
        for p in range(N_SLAB):
            sl = slice(p * LANES, (p + 1) * LANES)
            kr_ref[j, :, sl] = _rope_slab(k_ref[j, :, sl].astype(_F32), c_ref[j], s_ref[j], half).astype(_BF)

    tq, win = seg * tqs, seg * wks
    row = lax.broadcasted_iota(jnp.int32, (tq, win), 0)
    col = lax.broadcasted_iota(jnp.int32, (tq, win), 1)
    for i in range(m16 // tqs):
        a_q = i * tqs
        a_k = min(max(a_q - B_RADIUS // seg, 0), m16 - wks)
        qpos = seg * (a_q + (row & (tqs - 1))) + (row >> (tqs.bit_length() - 1))
        kpos = seg * (a_k + (col & (wks - 1))) + (col >> (wks.bit_length() - 1))
        valid = jnp.abs(qpos - kpos) <= B_RADIUS
        cq = jnp.concatenate([c_ref[j, a_q:a_q + tqs, :] for j in range(seg)], axis=0)
        sq = jnp.concatenate([s_ref[j, a_q:a_q + tqs, :] for j in range(seg)], axis=0)
        for p in range(N_SLAB):
            sl = slice(p * LANES, (p + 1) * LANES)
            q = jnp.concatenate([q_ref[j, a_q:a_q + tqs, sl] for j in range(seg)], axis=0).astype(_F32)
            q = _rope_slab(q, cq, sq, half) * Q_SCALE_LOG2
            kw = jnp.concatenate([kr_ref[j, a_k:a_k + wks, sl] for j in range(seg)], axis=0)
            vw = jnp.concatenate([v_ref[j, a_k:a_k + wks, sl] for j in range(seg)], axis=0)
            o, lse = _band_slab(q, kw, vw, valid)
            for j in range(seg):
                o_ref[j, a_q:a_q + tqs, sl] = o[j * tqs:(j + 1) * tqs].astype(o_ref.dtype)
                lse_ref[j, a_q:a_q + tqs, sl] = lse[j * tqs:(j + 1) * tqs]


def _attn_b4(perm, cos_p, sin_p):
    b, _, m16, cols = perm.shape
    pv = perm.reshape(b, 4, 4, m16, cols)
    cv = cos_p.reshape(4, 4, m16, LANES)
    sv = sin_p.reshape(4, 4, m16, LANES)
    blk = lambda cb: pl.BlockSpec((None, 4, None, m16, B_W), lambda bi, r: (bi, 0, r, 0, cb))
    tab = pl.BlockSpec((4, None, m16, LANES), lambda bi, r: (0, r, 0, 0))
    o, lse = pl.pallas_call(
        functools.partial(_attn_b4_kernel, m16=m16),
        grid=(b, 4),
        in_specs=[blk(0), blk(1), blk(2), tab, tab],
        out_specs=[blk(0), blk(0)],
        out_shape=[jax.ShapeDtypeStruct((b, 4, 4, m16, B_W), _BF), jax.ShapeDtypeStruct((b, 4, 4, m16, B_W), _F32)],
        scratch_shapes=[pltpu.VMEM((4, m16, B_W), _BF)],
        compiler_params=_cparams("parallel", "parallel"),
        name="attn_b4",
    )(pv, pv, pv, cv, sv)
    return o.reshape(b, PERM, m16, B_W), lse.reshape(b, PERM, m16, B_W)


def _attn_b16_kernel(q_ref, k_ref, v_ref, c_ref, s_ref, o_ref, lse_ref, *, m16, tq, win):
    half = ROPE_DIMS // 2
    row = lax.broadcasted_iota(jnp.int32, (tq, win), 0)
    col = lax.broadcasted_iota(jnp.int32, (tq, win), 1)

    def residue(r, carry):
        for i in range(m16 // tq):
            q0 = i * tq
            start = min(max(q0 - B_RADIUS, 0), m16 - win)
            valid = jnp.abs(row - col + (q0 - start)) <= B_RADIUS
            for p in range(N_SLAB):
                sl = slice(p * LANES, (p + 1) * LANES)
                q = _rope_slab(q_ref[r, q0:q0 + tq, sl].astype(_F32), c_ref[r, q0:q0 + tq, :], s_ref[r, q0:q0 + tq, :], half)
                kw = _rope_slab(k_ref[r, start:start + win, sl].astype(_F32), c_ref[r, start:start + win, :],
                                s_ref[r, start:start + win, :], half).astype(_BF)
                o, lse = _band_slab(q * Q_SCALE_LOG2, kw, v_ref[r, start:start + win, sl], valid)
                o_ref[r, q0:q0 + tq, sl] = o.astype(o_ref.dtype)
                lse_ref[r, q0:q0 + tq, sl] = lse
        return carry

    lax.fori_loop(0, PERM, residue, 0, unroll=2)


def _attn_b16(perm, cos_p, sin_p):
    b, _, m16, cols = perm.shape
    tq = min(128, m16)
    win = min(tq + 2 * B_RADIUS, m16)
    c0 = 3
    blk = lambda cb: pl.BlockSpec((None, PERM, m16, B_W), lambda bi: (bi, 0, 0, cb))
    return pl.pallas_call(
        functools.partial(_attn_b16_kernel, m16=m16, tq=tq, win=win),
        grid=(b,),
        in_specs=[blk(c0), blk(c0 + 1), blk(c0 + 2), _const_spec((PERM, m16, LANES)), _const_spec((PERM, m16, LANES))],
        out_specs=[blk(0), blk(0)],
        out_shape=[jax.ShapeDtypeStruct((b, PERM, m16, B_W), _BF), jax.ShapeDtypeStruct((b, PERM, m16, B_W), _F32)],
        compiler_params=_cparams("parallel"),
        name="attn_b16",
    )(perm, perm, perm, cos_p, sin_p)


def _even_out_kernel(x_ref, a_ref, o0_ref, l0_ref, o1_ref, l1_ref, o2_ref, l2_ref, w_ref, y_ref):
    l0 = l0_ref[...]
    l1 = _from_residue_major(l1_ref[...])
    l2 = _from_residue_major(l2_ref[...])
    o0 = o0_ref[...].astype(_F32)
    o1 = _from_residue_major(o1_ref[...].astype(_F32))
    o2 = _from_residue_major(o2_ref[...].astype(_F32))
    mx = jnp.maximum(jnp.maximum(l0, l1), l2)
    e0, e1, e2 = jnp.exp(l0 - mx), jnp.exp(l1 - mx), jnp.exp(l2 - mx)
    bo = (e0 * o0 + e1 * o1 + e2 * o2) / (e0 + e1 + e2)
    y = jnp.dot(a_ref[...], w_ref[:A_Q_W, :], preferred_element_type=_F32)
    y = y + jnp.dot(bo.astype(_BF), w_ref[A_Q_W:, :], preferred_element_type=_F32)
    y_ref[...] = x_ref[...] + y


def _even_out(x, a_o, o0, l0, o1, l1, o2, l2, w, *, s, tm):
    n, d = x.shape
    tpb = s // tm
    row = lambda width: pl.BlockSpec((tm, width), lambda i: (i, 0))
    prm = pl.BlockSpec((None, PERM, tm // PERM, B_W), lambda i: (i // tpb, 0, i % tpb, 0))
    return pl.pallas_call(
        _even_out_kernel,
        grid=(n // tm,),
        in_specs=[row(d), row(A_Q_W), row(B_W), row(B_W), prm, prm, prm, prm, _const_spec(w.shape)],
        out_specs=row(d),
        out_shape=jax.ShapeDtypeStruct((n, d), _F32),
        compiler_params=_cparams("parallel"),
        name="even_out",
    )(x, a_o, o0, l0, o1, l1, o2, l2, w)


def _ffn_kernel(*refs, chunk, final_norm, mixer_proj):
    if mixer_proj:
        x_ref, a_ref, wo_ref, g_ref, wg_ref, wu_ref, wd_ref, gf_ref, y_ref, h_ref, act_ref = refs
        x = x_ref[...] + jnp.dot(a_ref[...], wo_ref[...], preferred_element_type=_F32)
    else:
        x_ref, g_ref, wg_ref, wu_ref, wd_ref, gf_ref, y_ref, h_ref, act_ref = refs
        x = x_ref[...]
    h_ref[...] = _rms_rows(x, g_ref[...]).astype(_BF)
    for c in range(D_FF // chunk):
        sl = slice(c * chunk, (c + 1) * chunk)
        gate = jnp.dot(h_ref[...], wg_ref[:, sl], preferred_element_type=_F32)
        up = jnp.dot(h_ref[...], wu_ref[:, sl], preferred_element_type=_F32)
        act_ref[:, sl] = (gate * _sigmoid(gate) * up).astype(_BF)
    y = x + jnp.dot(act_ref[...], wd_ref[...], preferred_element_type=_F32)
    if final_norm:
        y = _rms_rows(y, gf_ref[...])
    y_ref[...] = y


def _ffn(x, g, w_gate_up, w_down, g_final, *, tm, final_norm, mixer_out=None, w_mixer=None):
    n, d = x.shape
    const = lambda shape, idx: pl.BlockSpec(shape, lambda i: idx, pipeline_mode=pl.Buffered(1))
    mixer_proj = mixer_out is not None
    row = lambda width: pl.BlockSpec((tm, width), lambda i: (i, 0))
    in_specs = [row(d)]
    args = [x]
    if mixer_proj:
        in_specs += [row(mixer_out.shape[1]), const(w_mixer.shape, (0, 0))]
        args += [mixer_out, w_mixer]
    in_specs += [
        const((1, d), (0, 0)),
        const((d, D_FF), (0, 0)),
        const((d, D_FF), (0, 1)),
        const((D_FF, d), (0, 0)),
        const((1, d), (0, 0)),
    ]
    args += [g, w_gate_up, w_gate_up, w_down, g_final]
    return pl.pallas_call(
        functools.partial(_ffn_kernel, chunk=256, final_norm=final_norm, mixer_proj=mixer_proj),
        grid=(n // tm,),
        in_specs=in_specs,
        out_specs=row(d),
        out_shape=jax.ShapeDtypeStruct((n, d), _F32),
        scratch_shapes=[pltpu.VMEM((tm, d), _BF), pltpu.VMEM((tm, D_FF), _BF)],
        compiler_params=_cparams("parallel"),
        name="ffn_mix" if mixer_proj else "ffn",
    )(*args)


def _conv_kernel(u_ref, w_ref, b_ref, o_ref):
    s = u_ref.shape[0]
    t = lax.broadcasted_iota(jnp.int32, (s, LANES), 0)
    for c in range(D_RNN // LANES):
        sl = slice(c * LANES, (c + 1) * LANES)
        u = u_ref[:, sl]
        um1 = jnp.where(t >= 1, pltpu.roll(u, 1, 0), 0.0)
        up1 = jnp.where(t < s - 1, pltpu.roll(u, s - 1, 0), 0.0)
        up2 = jnp.where(t < s - 2, pltpu.roll(u, s - 2, 0), 0.0)
        w = w_ref[:, sl]
        o_ref[:, sl] = um1 * w[0:1] + u * w[1:2] + up1 * w[2:3] + up2 * w[3:4] + b_ref[:, sl]


def _conv(u, w, bias):
    b, s, d = u.shape
    return pl.pallas_call(
        _conv_kernel,
        grid=(b,),
        in_specs=[pl.BlockSpec((None, s, d), lambda i: (i, 0, 0)), _const_spec(w.shape), _const_spec((1, d))],
        out_specs=pl.BlockSpec((None, s, d), lambda i: (i, 0, 0)),
        out_shape=jax.ShapeDtypeStruct((b, s, d), _F32),
        compiler_params=_cparams("parallel"),
        name="conv",
    )(u, w, bias)


SCAN_BATCH = 8
SCAN_UNROLL = 4


def _scan_kernel(*refs, tc, reverse, finalize):
    if finalize:
        u_ref, w_ref, ba_ref, bx_ref, lam_ref, hf_ref, gate_ref, o_ref, a_s, b_s, h_s, carry = refs
    else:
        u_ref, w_ref, ba_ref, bx_ref, lam_ref, o_ref, a_s, b_s, h_s, carry = refs
    rows = SCAN_BATCH * tc

    @pl.when(pl.program_id(1) == 0)
    def _():
        carry[...] = jnp.zeros_like(carry)

    for n in range(LRU_BLOCKS):
        sl = slice(n * LRU_BW, (n + 1) * LRU_BW)
        lam = lam_ref[:, sl]
        c2 = (-4.0 * LOG2E) * (jnp.maximum(-lam, 0.0) + jnp.log1p(jnp.exp(-jnp.abs(lam))))
        x = jnp.swapaxes(u_ref[:, :, sl], 0, 1).reshape(rows, LRU_BW)
        z = jnp.dot(x.astype(_BF), w_ref[n], preferred_element_type=_F32)
        th_a = jnp.tanh(z[:, :LRU_BW] + ba_ref[:, sl])
        th_x = jnp.tanh(z[:, LRU_BW:] + bx_ref[:, sl])
        a = jnp.exp2(c2 + c2 * th_a)
        a_s[n] = a
        v = 1.0 - a * a
        root = jnp.where(v > 0.0, v * lax.rsqrt(v), 0.0)
        b_s[n] = root * ((1.0 + th_x) * (0.5 * x))

    def step(i, hs):
        t = (tc - 1 - i) if reverse else i
        idx = pl.ds(pl.multiple_of(t * SCAN_BATCH, SCAN_BATCH), SCAN_BATCH)
        new = []
        for n in range(LRU_BLOCKS):
            h = a_s[n, idx, :] * hs[n] + b_s[n, idx, :]
            h_s[n, idx, :] = h
            new.append(h)
        return tuple(new)

    hs = lax.fori_loop(0, tc, step, tuple(carry[n] for n in range(LRU_BLOCKS)), unroll=SCAN_UNROLL)
    for n in range(LRU_BLOCKS):
        carry[n] = hs[n]

    for n in range(LRU_BLOCKS):
        sl = slice(n * LRU_BW, (n + 1) * LRU_BW)
        h = jnp.swapaxes(h_s[n].reshape(tc, SCAN_BATCH, LRU_BW), 0, 1)
        if finalize:
            o_ref[:, :, sl] = ((hf_ref[:, :, sl] + h) * gate_ref[:, :, sl].astype(_F32)).astype(o_ref.dtype)
        else:
            o_ref[:, :, sl] = h


def _scan(u, w, ba, bx, lam, *, tc, reverse, h_fwd=None, gate=None):
    b, s, d = u.shape
    nt = s // tc
    finalize = h_fwd is not None
    tmap = (lambda g, c: (g, nt - 1 - c, 0)) if reverse else (lambda g, c: (g, c, 0))
    blk = pl.BlockSpec((SCAN_BATCH, tc, d), tmap)
    in_specs = [blk, _const_spec(w.shape), _const_spec((1, d)), _const_spec((1, d)), _const_spec((1, d))]
    args = [u, w, ba, bx, lam]
    if finalize:
        in_specs += [blk, blk]
        args += [h_fwd, gate]
    work = pltpu.VMEM((LRU_BLOCKS, SCAN_BATCH * tc, LRU_BW), _F32)
    return pl.pallas_call(
        functools.partial(_scan_kernel, tc=tc, reverse=reverse, finalize=finalize),
        grid=(b // SCAN_BATCH, nt),
        in_specs=in_specs,
        out_specs=blk,
        out_shape=jax.ShapeDtypeStruct((b, s, d), _BF if finalize else _F32),
        scratch_shapes=[work, work, work, pltpu.VMEM((LRU_BLOCKS, SCAN_BATCH, LRU_BW), _F32)],
        compiler_params=_cparams("parallel", "arbitrary"),
        name="scan_bwd" if reverse else "scan_fwd",
    )(*args)


def _rope_tables(pos_lo, pos_hi, n_rot, theta, s):
    half = n_rot // 2
    inv = jnp.power(jnp.float32(theta), -jnp.arange(half, dtype=_F32) * (2.0 / n_rot))

    def part(pos):
        ang = pos.astype(_F32)[:, None] * inv[None, :]
        c, sn = jnp.cos(ang), jnp.sin(ang)
        return jnp.concatenate([c, c], axis=-1), jnp.concatenate([-sn, sn], axis=-1)

    cs, sns = [], []
    for pos in (pos_lo, pos_hi):
        if pos is not None:
            c, sn = part(pos)
            cs.append(c)
            sns.append(sn)
    width = sum(c.shape[-1] for c in cs)
    cs.append(jnp.ones((s, HEAD_DIM - width), _F32))
    sns.append(jnp.zeros((s, HEAD_DIM - width), _F32))
    c = jnp.concatenate(cs, axis=-1)
    sn = jnp.concatenate(sns, axis=-1)
    return jnp.concatenate([c, c], axis=-1), jnp.concatenate([sn, sn], axis=-1)


def _even_w_in_layout(w):
    d = w.shape[0]
    aq = w[:, :A_Q_W].reshape(d, A_Q_HEADS, HEAD_DIM)[:, jnp.array(A_HEAD_ORDER), :].reshape(d, A_Q_W)
    akv = w[:, A_Q_W:A_Q_W + 2 * LANES]
    pad = jnp.zeros((d, 1024 - A_Q_W - 2 * LANES), w.dtype)
    rest = w[:, A_Q_W + 2 * LANES:]
    w_nat = jnp.concatenate([aq, akv, pad, rest[:, :3 * B_W]], axis=1).astype(_BF)
    return w_nat, rest[:, 3 * B_W:].astype(_BF)


def _even_w_out_layout(w):
    d = w.shape[1]
    wa = w[:A_Q_W].reshape(A_Q_HEADS, HEAD_DIM, d)[jnp.array(A_HEAD_ORDER)].reshape(A_Q_W, d)
    return jnp.concatenate([wa, w[A_Q_W:]], axis=0).astype(_BF)


def kernel(x, mix_norm, ffn_norm, final_norm, even_w_in, even_q_norm, even_k_norm, even_w_out, odd_w_in, odd_conv_w, odd_conv_b, odd_gate_a_w, odd_gate_a_b, odd_gate_x_w, odd_gate_x_b, odd_lambda, odd_w_out, ffn_w_gate_up, ffn_w_down):
    b, s, d = x.shape
    n = b * s
    depth = mix_norm.shape[0]
    m16 = s // PERM
    assert d == D_MODEL and s % 512 == 0 and b % SCAN_BATCH == 0

    pos = jnp.arange(s, dtype=jnp.int32)
    cos_a, sin_a = _rope_tables(pos // GRID_W, pos % GRID_W, HEAD_DIM // 2, AXIAL_THETA, s)
    cos_b, sin_b = _rope_tables(pos, None, ROPE_DIMS, ROPE_THETA, s)
    cos_p = cos_b.reshape(m16, PERM, LANES).transpose(1, 0, 2)
    sin_p = sin_b.reshape(m16, PERM, LANES).transpose(1, 0, 2)

    tm = 512
    xf = x.reshape(n, d)
    for layer in range(depth):
        j = layer // 2
        g_mix = mix_norm[layer].reshape(1, d)
        if layer % 2 == 0:
            w_nat, w_perm = _even_w_in_layout(even_w_in[j])
            nat, perm = _even_in(xf, g_mix, w_nat, w_perm, b=b, s=s, tm=tm)
            nat3 = nat.reshape(b, s, NAT_COLS)
            gq = jnp.tile(even_q_norm[j], 2).reshape(1, LANES)
            gk = jnp.tile(even_k_norm[j], 2).reshape(1, LANES)
            a_o = _attn_a(nat3, cos_a, sin_a, gq, gk, tq=512).reshape(n, A_Q_W)
            o0, l0 = _attn_b1(nat3, cos_b, sin_b)
            o1, l1 = _attn_b4(perm, cos_p, sin_p)
            o2, l2 = _attn_b16(perm, cos_p, sin_p)
            xf = _even_out(xf, a_o, o0, l0, o1, l1, o2, l2, _even_w_out_layout(even_w_out[j]), s=s, tm=tm)
            mix = {}
        else:
            gate, u = _odd_in(xf, g_mix, odd_w_in[j].astype(_BF), tm=tm)
            uc = _conv(u.reshape(b, s, D_RNN), odd_conv_w[j], odd_conv_b[j].reshape(1, D_RNN))
            h = None
            for direction in range(2):
                w = (0.5 * jnp.concatenate([odd_gate_a_w[j, direction], odd_gate_x_w[j, direction]], axis=-1)).astype(_BF)
                h = _scan(uc, w, 0.5 * odd_gate_a_b[j, direction].reshape(1, D_RNN), 0.5 * odd_gate_x_b[j, direction].reshape(1, D_RNN),
                          odd_lambda[j, direction].reshape(1, D_RNN), tc=128, reverse=direction == 1,
                          h_fwd=h, gate=None if direction == 0 else gate.reshape(b, s, D_RNN))
            mix = dict(mixer_out=h.reshape(n, D_RNN), w_mixer=odd_w_out[j].astype(_BF))
        xf = _ffn(xf, ffn_norm[layer].reshape(1, d), ffn_w_gate_up[layer].astype(_BF), ffn_w_down[layer].astype(_BF),
                  final_norm.reshape(1, d), tm=tm, final_norm=layer == depth - 1, **mix)
    return xf.reshape(b, s, d)
```

```python
import functools

import jax
import jax.numpy as jnp
from jax import lax
from jax.experimental import pallas as pl
from jax.experimental.pallas import tpu as pltpu

D_MODEL = 1024
HEAD_DIM = 64
A_Q_HEADS = 8
A_KV_HEADS = 2
B_HEADS = 8
B_RADIUS = 64
GRID_W = 64
AXIAL_THETA = 10000.0
ROPE_THETA = 500000.0
ROPE_DIMS = HEAD_DIM // 4
D_RNN = D_MODEL
LRU_BLOCKS = 8
LRU_BW = D_RNN // LRU_BLOCKS
LRU_C = 8.0
D_FF = 2816
EPS = 1e-6
NEG_INF = -1e30
LOG2E = 1.4426950408889634
LN2 = 0.6931471805599453
Q_SCALE_LOG2 = HEAD_DIM ** -0.5 * LOG2E

LANES = 128
A_Q_W = A_Q_HEADS * HEAD_DIM
B_W = B_HEADS * HEAD_DIM
N_SLAB = B_W // LANES
A_CHAIN_ROWS = 256
A_SAFE_SCORE = 50.0
A_HEAD_ORDER = (0, 4, 1, 5, 2, 6, 3, 7)
NAT_COLS = A_Q_W + 3 * B_W + 2 * LANES
A_KV_BLOCK = (A_Q_W + 3 * B_W) // LANES
PERM = 16
PROJ_CHUNK = 512

VMEM_LIMIT = 56 * 1024 * 1024

_BF = jnp.bfloat16
_F32 = jnp.float32


def _cparams(*sem):
    return pltpu.CompilerParams(dimension_semantics=sem, vmem_limit_bytes=VMEM_LIMIT)


def _const_spec(shape):
    return pl.BlockSpec(shape, lambda *_: (0,) * len(shape), pipeline_mode=pl.Buffered(1))


def _rms_rows(x, g):
    ms = jnp.mean(x * x, axis=-1, keepdims=True)
    return x * lax.rsqrt(ms + EPS) * g


def _sigmoid(x):
    return 0.5 * (1.0 + jnp.tanh(0.5 * x))


def _lane(shape):
    return lax.broadcasted_iota(jnp.int32, shape, len(shape) - 1)


def _to_residue_major(y):
    rows, c = y.shape
    return jnp.swapaxes(y.reshape(rows // PERM, PERM, c), 0, 1).reshape(rows, c)


def _from_residue_major(y):
    p, m, c = y.shape
    return jnp.swapaxes(y, 0, 1).reshape(p * m, c)


def _even_in_kernel(x_ref, g_ref, wn_ref, wp_ref, nat_ref, perm_ref, hn_ref, hp_ref):
    tm = x_ref.shape[0]
    y = _rms_rows(x_ref[...], g_ref[...])
    hn_ref[...] = y.astype(_BF)
    hp_ref[...] = _to_residue_major(y).astype(_BF)
    cn = wn_ref.shape[1]
    for c0 in range(0, cn, PROJ_CHUNK):
        sl = slice(c0, min(c0 + PROJ_CHUNK, cn))
        nat_ref[:, sl] = jnp.dot(hn_ref[...], wn_ref[:, sl], preferred_element_type=_F32).astype(_BF)
    for c in range(wp_ref.shape[1] // PROJ_CHUNK):
        sl = slice(c * PROJ_CHUNK, (c + 1) * PROJ_CHUNK)
        acc = jnp.dot(hp_ref[...], wp_ref[:, sl], preferred_element_type=_F32)
        perm_ref[:, :, sl] = acc.reshape(PERM, tm // PERM, PROJ_CHUNK).astype(_BF)


def _even_in(x, g, w_nat, w_perm, *, b, s, tm):
    n, d = x.shape
    tpb = s // tm
    cn, cp = w_nat.shape[1], w_perm.shape[1]
    return pl.pallas_call(
        _even_in_kernel,
        grid=(n // tm,),
        in_specs=[pl.BlockSpec((tm, d), lambda i: (i, 0)), _const_spec((1, d)), _const_spec((d, cn)), _const_spec((d, cp))],
        out_specs=[
            pl.BlockSpec((tm, cn), lambda i: (i, 0)),
            pl.BlockSpec((None, PERM, tm // PERM, cp), lambda i: (i // tpb, 0, i % tpb, 0)),
        ],
        out_shape=[
            jax.ShapeDtypeStruct((n, cn), _BF),
            jax.ShapeDtypeStruct((b, PERM, s // PERM, cp), _BF),
        ],
        scratch_shapes=[pltpu.VMEM((tm, d), _BF), pltpu.VMEM((tm, d), _BF)],
        compiler_params=_cparams("parallel"),
        name="even_in",
    )(x, g, w_nat, w_perm)


def _gelu_tanh(x):
    return 0.5 * x * (1.0 + jnp.tanh(0.7978845608028654 * (x + 0.044715 * (x * x * x))))


def _odd_in_kernel(x_ref, g_ref, w_ref, gate_ref, u_ref, h_ref):
    h_ref[...] = _rms_rows(x_ref[...], g_ref[...]).astype(_BF)
    for c in range(D_RNN // PROJ_CHUNK):
        sl = slice(c * PROJ_CHUNK, (c + 1) * PROJ_CHUNK)
        acc = jnp.dot(h_ref[...], w_ref[:, sl], preferred_element_type=_F32)
        gate_ref[:, sl] = _gelu_tanh(acc).astype(gate_ref.dtype)
    for c in range(D_RNN // PROJ_CHUNK):
        sl = slice(c * PROJ_CHUNK, (c + 1) * PROJ_CHUNK)
        u_ref[:, sl] = jnp.dot(h_ref[...], w_ref[:, D_RNN + c * PROJ_CHUNK:D_RNN + (c + 1) * PROJ_CHUNK],
                               preferred_element_type=_F32)


def _odd_in(x, g, w, *, tm):
    n, d = x.shape
    return pl.pallas_call(
        _odd_in_kernel,
        grid=(n // tm,),
        in_specs=[pl.BlockSpec((tm, d), lambda i: (i, 0)), _const_spec((1, d)), _const_spec(w.shape)],
        out_specs=[pl.BlockSpec((tm, D_RNN), lambda i: (i, 0)), pl.BlockSpec((tm, D_RNN), lambda i: (i, 0))],
        out_shape=[jax.ShapeDtypeStruct((n, D_RNN), _BF), jax.ShapeDtypeStruct((n, D_RNN), _F32)],
        scratch_shapes=[pltpu.VMEM((tm, d), _BF)],
        compiler_params=_cparams("parallel"),
        name="odd_in",
    )(x, g, w)


def _rope_slab(x, cos, sin_signed, half):
    lane = _lane(x.shape)
    fwd = pltpu.roll(x, LANES - half, 1)
    bwd = pltpu.roll(x, half, 1)
    partner = jnp.where((lane % (2 * half)) < half, fwd, bwd)
    return x * cos + partner * sin_signed


def _head_rms_slab(x, g):
    lo = _lane(x.shape) < HEAD_DIM
    sq = x * x
    ms_lo = jnp.sum(jnp.where(lo, sq, 0.0), axis=-1, keepdims=True) * (1.0 / HEAD_DIM)
    ms_hi = jnp.sum(jnp.where(lo, 0.0, sq), axis=-1, keepdims=True) * (1.0 / HEAD_DIM)
    rs = jnp.where(lo, lax.rsqrt(ms_lo + EPS), lax.rsqrt(ms_hi + EPS))
    return x * rs * g


def _attn_a_kernel(q_ref, k_ref, v_ref, cq_ref, sq_ref, ck_ref, sk_ref, gq_ref, gk_ref, o_ref, kt_ref, *, tq):
    @pl.when(pl.program_id(1) == 0)
    def _():
        k = _head_rms_slab(k_ref[...].astype(_F32), gk_ref[...])
        k = _rope_slab(k, ck_ref[...], sk_ref[...], 16)
        kt_ref[...] = k.T.astype(_BF)

    lo = _lane((A_CHAIN_ROWS, LANES)) < HEAD_DIM
    gq = gq_ref[...]
    gk = gk_ref[...]
    bound_sq = (HEAD_DIM * Q_SCALE_LOG2) ** 2 * jnp.max(gq * gq) * jnp.max(gk * gk)
    bounded = bound_sq <= A_SAFE_SCORE * A_SAFE_SCORE

    def chains(subtract_max):
        v = v_ref[...]
        for r0 in range(0, tq, A_CHAIN_ROWS):
            rows = slice(r0, r0 + A_CHAIN_ROWS)
            cq = cq_ref[rows, :]
            sq = sq_ref[rows, :]
            for j in range(A_Q_W // LANES):
                cols = slice(j * LANES, (j + 1) * LANES)
                q = _rope_slab(_head_rms_slab(q_ref[rows, cols].astype(_F32), gq), cq, sq, 16) * Q_SCALE_LOG2
                outs = []
                for qm in (jnp.where(lo, q, 0.0), jnp.where(lo, 0.0, q)):
                    s = jnp.dot(qm.astype(_BF), kt_ref[...], preferred_element_type=_F32)
                    if subtract_max:
                        s = s - jnp.max(s, axis=-1, keepdims=True)
                    p = jnp.exp2(s)
                    l = jnp.sum(p, axis=-1, keepdims=True)
                    outs.append(jnp.dot(p.astype(_BF), v, preferred_element_type=_F32) / l)
                o_ref[rows, cols] = jnp.where(lo, outs[0], outs[1]).astype(o_ref.dtype)

    @pl.when(bounded)
    def _():
        chains(False)

    @pl.when(jnp.logical_not(bounded))
    def _():
        chains(True)


def _attn_a(nat, cos, sin, gq, gk, *, tq):
    b, s, _ = nat.shape
    return pl.pallas_call(
        functools.partial(_attn_a_kernel, tq=tq),
        grid=(b, s // tq),
        in_specs=[
            pl.BlockSpec((None, tq, A_Q_W), lambda bi, i: (bi, i, 0)),
            pl.BlockSpec((None, s, LANES), lambda bi, i: (bi, 0, A_KV_BLOCK)),
            pl.BlockSpec((None, s, LANES), lambda bi, i: (bi, 0, A_KV_BLOCK + 1)),
            pl.BlockSpec((tq, LANES), lambda bi, i: (i, 0)),
            pl.BlockSpec((tq, LANES), lambda bi, i: (i, 0)),
            pl.BlockSpec((s, LANES), lambda bi, i: (0, 0)),
            pl.BlockSpec((s, LANES), lambda bi, i: (0, 0)),
            pl.BlockSpec((1, LANES), lambda bi, i: (0, 0)),
            pl.BlockSpec((1, LANES), lambda bi, i: (0, 0)),
        ],
        out_specs=pl.BlockSpec((None, tq, A_Q_W), lambda bi, i: (bi, i, 0)),
        out_shape=jax.ShapeDtypeStruct((b, s, A_Q_W), _BF),
        scratch_shapes=[pltpu.VMEM((LANES, s), _BF)],
        compiler_params=_cparams("parallel", "arbitrary"),
        name="attn_a",
    )(nat, nat, nat, cos, sin, cos, sin, gq, gk)


def _band_slab(q, kw, vw, valid):
    tq = q.shape[0]
    lo = _lane(q.shape) < HEAD_DIM
    qm = jnp.concatenate([jnp.where(lo, q, 0.0), jnp.where(lo, 0.0, q)], axis=0).astype(_BF)
    s = lax.dot_general(qm, kw, (((1,), (1,)), ((), ())), preferred_element_type=_F32)
    s = jnp.where(jnp.concatenate([valid, valid], axis=0), s, NEG_INF)
    mx = jnp.max(s, axis=-1, keepdims=True)
    pr = jnp.exp2(s - mx)
    l = jnp.sum(pr, axis=-1, keepdims=True)
    o = jnp.dot(pr.astype(_BF), vw, preferred_element_type=_F32) / l
    lse = (mx + jnp.log2(l)) * LN2
    return jnp.where(lo, o[:tq], o[tq:]), jnp.where(lo, lse[:tq], lse[tq:])


def _attn_b1_kernel(q_ref, k_ref, v_ref, c_ref, s_ref, o_ref, lse_ref, kr_ref, *, m_len, tq, win):
    half = ROPE_DIMS // 2
    for p in range(N_SLAB):
        sl = slice(p * LANES, (p + 1) * LANES)
        kr_ref[:, sl] = _rope_slab(k_ref[:, sl].astype(_F32), c_ref[...], s_ref[...], half).astype(_BF)

    row = lax.broadcasted_iota(jnp.int32, (tq, win), 0)
    col = lax.broadcasted_iota(jnp.int32, (tq, win), 1)

    def block(i, carry):
        q0 = pl.multiple_of(i * tq, tq)
        start = pl.multiple_of(jnp.clip(q0 - B_RADIUS, 0, m_len - win), B_RADIUS)
        valid = jnp.abs(row - col + (q0 - start)) <= B_RADIUS
        cq = c_ref[pl.ds(q0, tq), :]
        sq = s_ref[pl.ds(q0, tq), :]
        for p in range(N_SLAB):
            sl = slice(p * LANES, (p + 1) * LANES)
            q = _rope_slab(q_ref[pl.ds(q0, tq), sl].astype(_F32), cq, sq, half) * Q_SCALE_LOG2
            o, lse = _band_slab(q, kr_ref[pl.ds(start, win), sl], v_ref[pl.ds(start, win), sl], valid)
            o_ref[pl.ds(q0, tq), sl] = o.astype(o_ref.dtype)
            lse_ref[pl.ds(q0, tq), sl] = lse
        return carry

    lax.fori_loop(0, m_len // tq, block, 0, unroll=4)


def _attn_b1(nat, cos, sin):
    b, s, _ = nat.shape
    tq = 128
    win = min(tq + 2 * B_RADIUS, s)
    c0 = A_Q_W // B_W
    o, lse = pl.pallas_call(
        functools.partial(_attn_b1_kernel, m_len=s, tq=tq, win=win),
        grid=(b,),
        in_specs=[
            pl.BlockSpec((None, s, B_W), lambda bi: (bi, 0, c0)),
            pl.BlockSpec((None, s, B_W), lambda bi: (bi, 0, c0 + 1)),
            pl.BlockSpec((None, s, B_W), lambda bi: (bi, 0, c0 + 2)),
            _const_spec((s, LANES)),
            _const_spec((s, LANES)),
        ],
        out_specs=[pl.BlockSpec((None, s, B_W), lambda bi: (bi, 0, 0))] * 2,
        out_shape=[jax.ShapeDtypeStruct((b, s, B_W), _BF), jax.ShapeDtypeStruct((b, s, B_W), _F32)],
        scratch_shapes=[pltpu.VMEM((s, B_W), _BF)],
        compiler_params=_cparams("parallel"),
        name="attn_b1",
    )(nat, nat, nat, cos, sin)
    return o.reshape(b * s, B_W), lse.reshape(b * s, B_W)


def _attn_b4_kernel(q_ref, k_ref, v_ref, c_ref, s_ref, o_ref, lse_ref, kr_ref, *, m16):
    half = ROPE_DIMS // 2
    seg = 4
    tqs = 32
    wks = min(tqs + 2 * (B_RADIUS // seg), m16)
    for j in range(seg):
        for p in range(N_SLAB):
            sl = slice(p * LANES, (p + 1) * LANES)
            kr_ref[j, :, sl] = _rope_slab(k_ref[j, :, sl].astype(_F32), c_ref[j], s_ref[j], half).astype(_BF)

    tq, win = seg * tqs, seg * wks
    row = lax.broadcasted_iota(jnp.int32, (tq, win), 0)
    col = lax.broadcasted_iota(jnp.int32, (tq, win), 1)
    for i in range(m16 // tqs):
        a_q = i * tqs
        a_k = min(max(a_q - B_RADIUS // seg, 0), m16 - wks)
        qpos = seg * (a_q + (row & (tqs - 1))) + (row >> (tqs.bit_length() - 1))
        kpos = seg * (a_k + (col & (wks - 1))) + (col >> (wks.bit_length() - 1))
        valid = jnp.abs(qpos - kpos) <= B_RADIUS
        cq = jnp.concatenate([c_ref[j, a_q:a_q + tqs, :] for j in range(seg)], axis=0)
        sq = jnp.concatenate([s_ref[j, a_q:a_q + tqs, :] for j in range(seg)], axis=0)
        for p in range(N_SLAB):
            sl = slice(p * LANES, (p + 1) * LANES)
            q = jnp.concatenate([q_ref[j, a_q:a_q + tqs, sl] for j in range(seg)], axis=0).astype(_F32)
            q = _rope_slab(q, cq, sq, half) * Q_SCALE_LOG2
            kw = jnp.concatenate([kr_ref[j, a_k:a_k + wks, sl] for j in range(seg)], axis=0)
            vw = jnp.concatenate([v_ref[j, a_k:a_k + wks, sl] for j in range(seg)], axis=0)
            o, lse = _band_slab(q, kw, vw, valid)
            for j in range(seg):
                o_ref[j, a_q:a_q + tqs, sl] = o[j * tqs:(j + 1) * tqs].astype(o_ref.dtype)
                lse_ref[j, a_q:a_q + tqs, sl] = lse[j * tqs:(j + 1) * tqs]


def _attn_b4(perm, cos_p, sin_p):
    b, _, m16, cols = perm.shape
    pv = perm.reshape(b, 4, 4, m16, cols)
    cv = cos_p.reshape(4, 4, m16, LANES)
    sv = sin_p.reshape(4, 4, m16, LANES)
    blk = lambda cb: pl.BlockSpec((None, 4, None, m16, B_W), lambda bi, r: (bi, 0, r, 0, cb))
    tab = pl.BlockSpec((4, None, m16, LANES), lambda bi, r: (0, r, 0, 0))
    o, lse = pl.pallas_call(
        functools.partial(_attn_b4_kernel, m16=m16),
        grid=(b, 4),
        in_specs=[blk(0), blk(1), blk(2), tab, tab],
        out_specs=[blk(0), blk(0)],
        out_shape=[jax.ShapeDtypeStruct((b, 4, 4, m16, B_W), _BF), jax.ShapeDtypeStruct((b, 4, 4, m16, B_W), _F32)],
        scratch_shapes=[pltpu.VMEM((4, m16, B_W), _BF)],
        compiler_params=_cparams("parallel", "parallel"),
        name="attn_b4",
    )(pv, pv, pv, cv, sv)
    return o.reshape(b, PERM, m16, B_W), lse.reshape(b, PERM, m16, B_W)


def _attn_b16_kernel(q_ref, k_ref, v_ref, c_ref, s_ref, o_ref, lse_ref, *, m16, tq, win):
    half = ROPE_DIMS // 2
    row = lax.broadcasted_iota(jnp.int32, (tq, win), 0)
    col = lax.broadcasted_iota(jnp.int32, (tq, win), 1)

    def residue(r, carry):
        for i in range(m16 // tq):
            q0 = i * tq
            start = min(max(q0 - B_RADIUS, 0), m16 - win)
            valid = jnp.abs(row - col + (q0 - start)) <= B_RADIUS
            for p in range(N_SLAB):
                sl = slice(p * LANES, (p + 1) * LANES)
                q = _rope_slab(q_ref[r, q0:q0 + tq, sl].astype(_F32), c_ref[r, q0:q0 + tq, :], s_ref[r, q0:q0 + tq, :], half)
                kw = _rope_slab(k_ref[r, start:start + win, sl].astype(_F32), c_ref[r, start:start + win, :],
                                s_ref[r, start:start + win, :], half).astype(_BF)
                o, lse = _band_slab(q * Q_SCALE_LOG2, kw, v_ref[r, start:start + win, sl], valid)
                o_ref[r, q0:q0 + tq, sl] = o.astype(o_ref.dtype)
                lse_ref[r, q0:q0 + tq, sl] = lse
        return carry

    lax.fori_loop(0, PERM, residue, 0, unroll=2)


def _attn_b16(perm, cos_p, sin_p):
    b, _, m16, cols = perm.shape
    tq = min(128, m16)
    win = min(tq + 2 * B_RADIUS, m16)
    c0 = 3
    blk = lambda cb: pl.BlockSpec((None, PERM, m16, B_W), lambda bi: (bi, 0, 0, cb))
    return pl.pallas_call(
        functools.partial(_attn_b16_kernel, m16=m16, tq=tq, win=win),
        grid=(b,),
        in_specs=[blk(c0), blk(c0 + 1), blk(c0 + 2), _const_spec((PERM, m16, LANES)), _const_spec((PERM, m16, LANES))],
        out_specs=[blk(0), blk(0)],
        out_shape=[jax.ShapeDtypeStruct((b, PERM, m16, B_W), _BF), jax.ShapeDtypeStruct((b, PERM, m16, B_W), _F32)],
        compiler_params=_cparams("parallel"),
        name="attn_b16",
    )(perm, perm, perm, cos_p, sin_p)


def _even_out_kernel(x_ref, a_ref, o0_ref, l0_ref, o1_ref, l1_ref, o2_ref, l2_ref, w_ref, y_ref):
    l1, l2 = l1_ref[...], l2_ref[...]
    m12 = jnp.maximum(l1, l2)
    e1, e2 = jnp.exp(l1 - m12), jnp.exp(l2 - m12)
    den12 = e1 + e2
    o12 = _from_residue_major((e1 * o1_ref[...].astype(_F32) + e2 * o2_ref[...].astype(_F32)) / den12)
    l12 = _from_residue_major(m12 + jnp.log(den12))
    l0 = l0_ref[...]
    mx = jnp.maximum(l0, l12)
    e0, e12 = jnp.exp(l0 - mx), jnp.exp(l12 - mx)
    bo = (e0 * o0_ref[...].astype(_F32) + e12 * o12) / (e0 + e12)
    y = jnp.dot(a_ref[...], w_ref[:A_Q_W, :], preferred_element_type=_F32)
    y = y + jnp.dot(bo.astype(_BF), w_ref[A_Q_W:, :], preferred_element_type=_F32)
    y_ref[...] = x_ref[...] + y


def _even_out(x, a_o, o0, l0, o1, l1, o2, l2, w, *, s, tm):
    n, d = x.shape
    tpb = s // tm
    row = lambda width: pl.BlockSpec((tm, width), lambda i: (i, 0))
    prm = pl.BlockSpec((None, PERM, tm // PERM, B_W), lambda i: (i // tpb, 0, i % tpb, 0))
    return pl.pallas_call(
        _even_out_kernel,
        grid=(n // tm,),
        in_specs=[row(d), row(A_Q_W), row(B_W), row(B_W), prm, prm, prm, prm, _const_spec(w.shape)],
        out_specs=row(d),
        out_shape=jax.ShapeDtypeStruct((n, d), _F32),
        compiler_params=_cparams("parallel"),
        name="even_out",
    )(x, a_o, o0, l0, o1, l1, o2, l2, w)


def _ffn_kernel(*refs, chunk, final_norm, mixer_proj):
    if mixer_proj:
        x_ref, hs_ref, gate_ref, wo_ref, g_ref, wg_ref, wu_ref, wd_ref, gf_ref, y_ref, h_ref, act_ref = refs
        rows = h_ref.shape[0]
        hs = jnp.swapaxes(hs_ref[...].astype(_F32), 0, 1).reshape(rows, D_RNN)
        a = (hs * gate_ref[...].reshape(rows, D_RNN).astype(_F32)).astype(_BF)
        x = x_ref[...].reshape(rows, D_MODEL) + jnp.dot(a, wo_ref[...], preferred_element_type=_F32)
    else:
        x_ref, g_ref, wg_ref, wu_ref, wd_ref, gf_ref, y_ref, h_ref, act_ref = refs
        x = x_ref[...]
    h_ref[...] = _rms_rows(x, g_ref[...]).astype(_BF)
    for c in range(D_FF // chunk):
        sl = slice(c * chunk, (c + 1) * chunk)
        gate = jnp.dot(h_ref[...], wg_ref[:, sl], preferred_element_type=_F32)
        up = jnp.dot(h_ref[...], wu_ref[:, sl], preferred_element_type=_F32)
        act_ref[:, sl] = (gate * _sigmoid(gate) * up).astype(_BF)
    y = x + jnp.dot(act_ref[...], wd_ref[...], preferred_element_type=_F32)
    if final_norm:
        y = _rms_rows(y, gf_ref[...])
    y_ref[...] = y.reshape(y_ref.shape)


def _ffn(x, g, w_gate_up, w_down, g_final, *, tm, final_norm, hsum_tm=None, gate=None, w_mixer=None):
    b, s, d = x.shape
    mixer_proj = hsum_tm is not None
    const = lambda shape, *idx: pl.BlockSpec(shape, lambda *_: idx, pipeline_mode=pl.Buffered(1))
    weights = [const((1, d), 0, 0), const((d, D_FF), 0, 0), const((d, D_FF), 0, 1), const((D_FF, d), 0, 0), const((1, d), 0, 0)]
    w_args = [g, w_gate_up, w_gate_up, w_down, g_final]
    if mixer_proj:
        tt = tm // SCAN_BATCH
        grid = (b // SCAN_BATCH, s // tt)
        xs = pl.BlockSpec((SCAN_BATCH, tt, d), lambda gi, c: (gi, c, 0))
        in_specs = [xs, pl.BlockSpec((None, tt, SCAN_BATCH, d), lambda gi, c: (gi, c, 0, 0)), xs, const(w_mixer.shape, 0, 0)] + weights
        args = [x, hsum_tm, gate, w_mixer] + w_args
        sem = ("parallel", "parallel")
    else:
        x = x.reshape(b * s, d)
        grid = (b * s // tm,)
        xs = pl.BlockSpec((tm, d), lambda i: (i, 0))
        in_specs = [xs] + weights
        args = [x] + w_args
        sem = ("parallel",)
    y = pl.pallas_call(
        functools.partial(_ffn_kernel, chunk=256, final_norm=final_norm, mixer_proj=mixer_proj),
        grid=grid,
        in_specs=in_specs,
        out_specs=xs,
        out_shape=jax.ShapeDtypeStruct(x.shape, _F32),
        scratch_shapes=[pltpu.VMEM((tm, d), _BF), pltpu.VMEM((tm, D_FF), _BF)],
        compiler_params=_cparams(*sem),
        name="ffn_mix" if mixer_proj else "ffn",
    )(*args)
    return y.reshape(b, s, d)


SCAN_BATCH = 8
SCAN_UNROLL = 4
HALO = 8


def _conv_kernel(u_ref, prev_ref, next_ref, w_ref, b_ref, o_ref, *, tc):
    c = pl.program_id(1)
    has_prev = (c > 0).astype(_F32)
    has_next = (c < pl.num_programs(1) - 1).astype(_F32)
    t = lax.broadcasted_iota(jnp.int32, (tc, LANES), 0)
    for cb in range(D_RNN // LANES):
        sl = slice(cb * LANES, (cb + 1) * LANES)
        w = w_ref[:, sl]
        bias = b_ref[:, sl]
        rows = []
        for bi in range(SCAN_BATCH):
            u = u_ref[bi, :, sl]
            p_last = prev_ref[bi, HALO - 1:HALO, sl] * has_prev
            n0 = next_ref[bi, 0:1, sl] * has_next
            n1 = next_ref[bi, 1:2, sl] * has_next
            um1 = jnp.where(t == 0, p_last, pltpu.roll(u, 1, 0))
            up1 = jnp.where(t == tc - 1, n0, pltpu.roll(u, tc - 1, 0))
            up2 = jnp.where(t == tc - 2, n0, jnp.where(t == tc - 1, n1, pltpu.roll(u, tc - 2, 0)))
            rows.append(um1 * w[0:1] + u * w[1:2] + up1 * w[2:3] + up2 * w[3:4] + bias)
        o_ref[:, :, sl] = jnp.swapaxes(jnp.stack(rows, axis=0), 0, 1)


def _conv(u, w, bias, *, tc):
    b, s, d = u.shape
    last = s // HALO - 1
    per = tc // HALO
    return pl.pallas_call(
        functools.partial(_conv_kernel, tc=tc),
        grid=(b // SCAN_BATCH, s // tc),
        in_specs=[
            pl.BlockSpec((SCAN_BATCH, tc, d), lambda g, c: (g, c, 0)),
            pl.BlockSpec((SCAN_BATCH, HALO, d), lambda g, c: (g, jnp.maximum(c * per - 1, 0), 0)),
            pl.BlockSpec((SCAN_BATCH, HALO, d), lambda g, c: (g, jnp.minimum((c + 1) * per, last), 0)),
            _const_spec(w.shape),
            _const_spec((1, d)),
        ],
        out_specs=pl.BlockSpec((None, tc, SCAN_BATCH, d), lambda g, c: (g, c, 0, 0)),
        out_shape=jax.ShapeDtypeStruct((b // SCAN_BATCH, s, SCAN_BATCH, d), _F32),
        compiler_params=_cparams("parallel", "parallel"),
        name="conv",
    )(u, u, u, w, bias)


def _scan_kernel(*refs, tc, reverse, finalize):
    if finalize:
        u_ref, w_ref, ba_ref, bx_ref, lam_ref, hf_ref, o_ref, a_s, b_s, h_s, carry = refs
    else:
        u_ref, w_ref, ba_ref, bx_ref, lam_ref, o_ref, a_s, b_s, h_s, carry = refs
    rows = SCAN_BATCH * tc

    @pl.when(pl.program_id(1) == 0)
    def _():
        carry[...] = jnp.zeros_like(carry)

    for n in range(LRU_BLOCKS):
        sl = slice(n * LRU_BW, (n + 1) * LRU_BW)
        lam = lam_ref[:, sl]
        c2 = (-4.0 * LOG2E) * (jnp.maximum(-lam, 0.0) + jnp.log1p(jnp.exp(-jnp.abs(lam))))
        xh = u_ref[:, :, sl].reshape(rows, LRU_BW)
        z = jnp.dot(xh.astype(_BF), w_ref[n], preferred_element_type=_F32)
        th_a = jnp.tanh(z[:, :LRU_BW] + ba_ref[:, sl])
        th_x = jnp.tanh(z[:, LRU_BW:] + bx_ref[:, sl])
        a = jnp.exp2(c2 + c2 * th_a)
        a_s[n] = a
        v = 1.0 - a * a
        root = jnp.where(v > 0.0, v * lax.rsqrt(v), 0.0)
        b_s[n] = root * ((1.0 + th_x) * xh)

    def step(i, hs):
        t = (tc - 1 - i) if reverse else i
        idx = pl.ds(pl.multiple_of(t * SCAN_BATCH, SCAN_BATCH), SCAN_BATCH)
        new = []
        for n in range(LRU_BLOCKS):
            h = a_s[n, idx, :] * hs[n] + b_s[n, idx, :]
            h_s[n, idx, :] = h
            new.append(h)
        return tuple(new)

    hs = lax.fori_loop(0, tc, step, tuple(carry[n] for n in range(LRU_BLOCKS)), unroll=SCAN_UNROLL)
    for n in range(LRU_BLOCKS):
        carry[n] = hs[n]

    for n in range(LRU_BLOCKS):
        sl = slice(n * LRU_BW, (n + 1) * LRU_BW)
        h = h_s[n].reshape(tc, SCAN_BATCH, LRU_BW)
        if finalize:
            o_ref[:, :, sl] = (hf_ref[:, :, sl] + h).astype(o_ref.dtype)
        else:
            o_ref[:, :, sl] = h


def _scan(u, w, ba, bx, lam, *, tc, reverse, h_fwd=None):
    g8, s, _, d = u.shape
    nt = s // tc
    finalize = h_fwd is not None
    tmap = (lambda g, c: (g, nt - 1 - c, 0, 0)) if reverse else (lambda g, c: (g, c, 0, 0))
    blk = pl.BlockSpec((None, tc, SCAN_BATCH, d), tmap)
    in_specs = [blk, _const_spec(w.shape), _const_spec((1, d)), _const_spec((1, d)), _const_spec((1, d))]
    args = [u, w, ba, bx, lam]
    if finalize:
        in_specs += [blk]
        args += [h_fwd]
    work = pltpu.VMEM((LRU_BLOCKS, SCAN_BATCH * tc, LRU_BW), _F32)
    return pl.pallas_call(
        functools.partial(_scan_kernel, tc=tc, reverse=reverse, finalize=finalize),
        grid=(g8, nt),
        in_specs=in_specs,
        out_specs=blk,
        out_shape=jax.ShapeDtypeStruct(u.shape, _BF if finalize else _F32),
        scratch_shapes=[work, work, work, pltpu.VMEM((LRU_BLOCKS, SCAN_BATCH, LRU_BW), _F32)],
        compiler_params=_cparams("parallel", "arbitrary"),
        name="scan_bwd" if reverse else "scan_fwd",
    )(*args)


def _rope_tables(pos_lo, pos_hi, n_rot, theta, s):
    half = n_rot // 2
    inv = jnp.power(jnp.float32(theta), -jnp.arange(half, dtype=_F32) * (2.0 / n_rot))

    def part(pos):
        ang = pos.astype(_F32)[:, None] * inv[None, :]
        c, sn = jnp.cos(ang), jnp.sin(ang)
        return jnp.concatenate([c, c], axis=-1), jnp.concatenate([-sn, sn], axis=-1)

    cs, sns = [], []
    for pos in (pos_lo, pos_hi):
        if pos is not None:
            c, sn = part(pos)
            cs.append(c)
            sns.append(sn)
    width = sum(c.shape[-1] for c in cs)
    cs.append(jnp.ones((s, HEAD_DIM - width), _F32))
    sns.append(jnp.zeros((s, HEAD_DIM - width), _F32))
    c = jnp.concatenate(cs, axis=-1)
    sn = jnp.concatenate(sns, axis=-1)
    return jnp.concatenate([c, c], axis=-1), jnp.concatenate([sn, sn], axis=-1)


def _even_w_in_layout(w):
    d = w.shape[0]
    aq = w[:, :A_Q_W].reshape(d, A_Q_HEADS, HEAD_DIM)[:, jnp.array(A_HEAD_ORDER), :].reshape(d, A_Q_W)
    akv = w[:, A_Q_W:A_Q_W + 2 * LANES]
    rest = w[:, A_Q_W + 2 * LANES:]
    w_nat = jnp.concatenate([aq, rest[:, :3 * B_W], akv], axis=1).astype(_BF)
    return w_nat, rest[:, 3 * B_W:].astype(_BF)


def _even_w_out_layout(w):
    d = w.shape[1]
    wa = w[:A_Q_W].reshape(A_Q_HEADS, HEAD_DIM, d)[jnp.array(A_HEAD_ORDER)].reshape(A_Q_W, d)
    return jnp.concatenate([wa, w[A_Q_W:]], axis=0).astype(_BF)


def kernel(x, mix_norm, ffn_norm, final_norm, even_w_in, even_q_norm, even_k_norm, even_w_out, odd_w_in, odd_conv_w, odd_conv_b, odd_gate_a_w, odd_gate_a_b, odd_gate_x_w, odd_gate_x_b, odd_lambda, odd_w_out, ffn_w_gate_up, ffn_w_down):
    b, s, d = x.shape
    n = b * s
    depth = mix_norm.shape[0]
    m16 = s // PERM
    assert d == D_MODEL and s % 512 == 0 and b % SCAN_BATCH == 0

    pos = jnp.arange(s, dtype=jnp.int32)
    cos_a, sin_a = _rope_tables(pos // GRID_W, pos % GRID_W, HEAD_DIM // 2, AXIAL_THETA, s)
    cos_b, sin_b = _rope_tables(pos, None, ROPE_DIMS, ROPE_THETA, s)
    cos_p = cos_b.reshape(m16, PERM, LANES).transpose(1, 0, 2)
    sin_p = sin_b.reshape(m16, PERM, LANES).transpose(1, 0, 2)

    tm = 512
    xf = x.reshape(n, d)
    for layer in range(depth):
        j = layer // 2
        g_mix = mix_norm[layer].reshape(1, d)
        if layer % 2 == 0:
            w_nat, w_perm = _even_w_in_layout(even_w_in[j])
            nat, perm = _even_in(xf, g_mix, w_nat, w_perm, b=b, s=s, tm=tm)
            nat3 = nat.reshape(b, s, NAT_COLS)
            gq = jnp.tile(even_q_norm[j], 2).reshape(1, LANES)
            gk = jnp.tile(even_k_norm[j], 2).reshape(1, LANES)
            a_o = _attn_a(nat3, cos_a, sin_a, gq, gk, tq=512).reshape(n, A_Q_W)
            o0, l0 = _attn_b1(nat3, cos_b, sin_b)
            o1, l1 = _attn_b4(perm, cos_p, sin_p)
            o2, l2 = _attn_b16(perm, cos_p, sin_p)
            xf = _even_out(xf, a_o, o0, l0, o1, l1, o2, l2, _even_w_out_layout(even_w_out[j]), s=s, tm=tm)
            mix = {}
        else:
            gate, u = _odd_in(xf, g_mix, odd_w_in[j].astype(_BF), tm=tm)
            uc = _conv(u.reshape(b, s, D_RNN), 0.5 * odd_conv_w[j], 0.5 * odd_conv_b[j].reshape(1, D_RNN), tc=128)
            h = None
            for direction in range(2):
                w = jnp.concatenate([odd_gate_a_w[j, direction], odd_gate_x_w[j, direction]], axis=-1).astype(_BF)
                h = _scan(uc, w, 0.5 * odd_gate_a_b[j, direction].reshape(1, D_RNN), 0.5 * odd_gate_x_b[j, direction].reshape(1, D_RNN),
                          odd_lambda[j, direction].reshape(1, D_RNN), tc=128, reverse=direction == 1, h_fwd=h)
            mix = dict(hsum_tm=h, gate=gate.reshape(b, s, D_RNN), w_mixer=odd_w_out[j].astype(_BF))
        xf = _ffn(xf.reshape(b, s, d), ffn_norm[layer].reshape(1, d), ffn_w_gate_up[layer].astype(_BF), ffn_w_down[layer].astype(_BF),
                  final_norm.reshape(1, d), tm=tm, final_norm=layer == depth - 1, **mix).reshape(n, d)
    return xf.reshape(b, s, d)
```

```python
import functools

import jax
import jax.numpy as jnp
from jax import lax
from jax.experimental import pallas as pl
from jax.experimental.pallas import tpu as pltpu

D_MODEL = 1024
HEAD_DIM = 64
A_Q_HEADS = 8
A_KV_HEADS = 2
B_HEADS = 8
B_RADIUS = 64
GRID_W = 64
AXIAL_THETA = 10000.0
ROPE_THETA = 500000.0
ROPE_DIMS = HEAD_DIM // 4
D_RNN = D_MODEL
LRU_BLOCKS = 8
LRU_BW = D_RNN // LRU_BLOCKS
LRU_C = 8.0
D_FF = 2816
EPS = 1e-6
NEG_INF = -1e30
LOG2E = 1.4426950408889634
LN2 = 0.6931471805599453
Q_SCALE_LOG2 = HEAD_DIM ** -0.5 * LOG2E

LANES = 128
A_Q_W = A_Q_HEADS * HEAD_DIM
B_W = B_HEADS * HEAD_DIM
N_SLAB = B_W // LANES
A_CHAIN_ROWS = 256
A_SAFE_SCORE = 50.0
A_HEAD_ORDER = (0, 4, 1, 5, 2, 6, 3, 7)
NAT_COLS = A_Q_W + 3 * B_W + 2 * LANES
A_KV_BLOCK = (A_Q_W + 3 * B_W) // LANES
PERM = 16
PROJ_CHUNK = 512

VMEM_LIMIT = 56 * 1024 * 1024

_BF = jnp.bfloat16
_F32 = jnp.float32


def _cparams(*sem):
    return pltpu.CompilerParams(dimension_semantics=sem, vmem_limit_bytes=VMEM_LIMIT)


def _const_spec(shape):
    return pl.BlockSpec(shape, lambda *_: (0,) * len(shape), pipeline_mode=pl.Buffered(1))


def _rms_rows(x, g):
    ms = jnp.mean(x * x, axis=-1, keepdims=True)
    return x * lax.rsqrt(ms + EPS) * g


def _sigmoid(x):
    return 0.5 * (1.0 + jnp.tanh(0.5 * x))


def _lane(shape):
    return lax.broadcasted_iota(jnp.int32, shape, len(shape) - 1)


def _to_residue_major(y):
    rows, c = y.shape
    return jnp.swapaxes(y.reshape(rows // PERM, PERM, c), 0, 1).reshape(rows, c)


def _from_residue_major(y):
    p, m, c = y.shape
    return jnp.swapaxes(y, 0, 1).reshape(p * m, c)


def _even_in_kernel(x_ref, g_ref, wn_ref, wp_ref, nat_ref, perm_ref, hn_ref, hp_ref):
    tm = x_ref.shape[0]
    y = _rms_rows(x_ref[...], g_ref[...])
    hn_ref[...] = y.astype(_BF)
    hp_ref[...] = _to_residue_major(y).astype(_BF)
    cn = wn_ref.shape[1]
    for c0 in range(0, cn, PROJ_CHUNK):
        sl = slice(c0, min(c0 + PROJ_CHUNK, cn))
        nat_ref[:, sl] = jnp.dot(hn_ref[...], wn_ref[:, sl], preferred_element_type=_F32).astype(_BF)
    for c in range(wp_ref.shape[1] // PROJ_CHUNK):
        sl = slice(c * PROJ_CHUNK, (c + 1) * PROJ_CHUNK)
        acc = jnp.dot(hp_ref[...], wp_ref[:, sl], preferred_element_type=_F32)
        perm_ref[:, :, sl] = acc.reshape(PERM, tm // PERM, PROJ_CHUNK).astype(_BF)


def _even_in(x, g, w_nat, w_perm, *, b, s, tm):
    n, d = x.shape
    tpb = s // tm
    cn, cp = w_nat.shape[1], w_perm.shape[1]
    return pl.pallas_call(
        _even_in_kernel,
        grid=(n // tm,),
        in_specs=[pl.BlockSpec((tm, d), lambda i: (i, 0)), _const_spec((1, d)), _const_spec((d, cn)), _const_spec((d, cp))],
        out_specs=[
            pl.BlockSpec((tm, cn), lambda i: (i, 0)),
            pl.BlockSpec((None, PERM, tm // PERM, cp), lambda i: (i // tpb, 0, i % tpb, 0)),
        ],
        out_shape=[
            jax.ShapeDtypeStruct((n, cn), _BF),
            jax.ShapeDtypeStruct((b, PERM, s // PERM, cp), _BF),
        ],
        scratch_shapes=[pltpu.VMEM((tm, d), _BF), pltpu.VMEM((tm, d), _BF)],
        compiler_params=_cparams("parallel"),
        name="even_in",
    )(x, g, w_nat, w_perm)


def _gelu_tanh(x):
    return 0.5 * x * (1.0 + jnp.tanh(0.7978845608028654 * (x + 0.044715 * (x * x * x))))


def _odd_in_kernel(x_ref, g_ref, w_ref, gate_ref, u_ref, h_ref):
    h_ref[...] = _rms_rows(x_ref[...], g_ref[...]).astype(_BF)
    for c in range(D_RNN // PROJ_CHUNK):
        sl = slice(c * PROJ_CHUNK, (c + 1) * PROJ_CHUNK)
        acc = jnp.dot(h_ref[...], w_ref[:, sl], preferred_element_type=_F32)
        gate_ref[:, sl] = _gelu_tanh(acc).astype(gate_ref.dtype)
    for c in range(D_RNN // PROJ_CHUNK):
        sl = slice(c * PROJ_CHUNK, (c + 1) * PROJ_CHUNK)
        u_ref[:, sl] = jnp.dot(h_ref[...], w_ref[:, D_RNN + c * PROJ_CHUNK:D_RNN + (c + 1) * PROJ_CHUNK],
                               preferred_element_type=_F32)


def _odd_in(x, g, w, *, tm):
    n, d = x.shape
    return pl.pallas_call(
        _odd_in_kernel,
        grid=(n // tm,),
        in_specs=[pl.BlockSpec((tm, d), lambda i: (i, 0)), _const_spec((1, d)), _const_spec(w.shape)],
        out_specs=[pl.BlockSpec((tm, D_RNN), lambda i: (i, 0)), pl.BlockSpec((tm, D_RNN), lambda i: (i, 0))],
        out_shape=[jax.ShapeDtypeStruct((n, D_RNN), _BF), jax.ShapeDtypeStruct((n, D_RNN), _F32)],
        scratch_shapes=[pltpu.VMEM((tm, d), _BF)],
        compiler_params=_cparams("parallel"),
        name="odd_in",
    )(x, g, w)


def _rope_slab(x, cos, sin_signed, half):
    lane = _lane(x.shape)
    fwd = pltpu.roll(x, LANES - half, 1)
    bwd = pltpu.roll(x, half, 1)
    partner = jnp.where((lane % (2 * half)) < half, fwd, bwd)
    return x * cos + partner * sin_signed


def _head_rms_slab(x, g):
    lo = _lane(x.shape) < HEAD_DIM
    sq = x * x
    ms_lo = jnp.sum(jnp.where(lo, sq, 0.0), axis=-1, keepdims=True) * (1.0 / HEAD_DIM)
    ms_hi = jnp.sum(jnp.where(lo, 0.0, sq), axis=-1, keepdims=True) * (1.0 / HEAD_DIM)
    rs = jnp.where(lo, lax.rsqrt(ms_lo + EPS), lax.rsqrt(ms_hi + EPS))
    return x * rs * g


def _attn_a_kernel(q_ref, k_ref, v_ref, cq_ref, sq_ref, ck_ref, sk_ref, gq_ref, gk_ref,
                   q16_ref, k16_ref, v16_ref, c16_ref, s16_ref, o_ref, o16_ref, l16_ref, kt_ref, *, tq):
    @pl.when(pl.program_id(1) == 0)
    def _():
        k = _head_rms_slab(k_ref[...].astype(_F32), gk_ref[...])
        k = _rope_slab(k, ck_ref[...], sk_ref[...], 16)
        kt_ref[...] = k.T.astype(_BF)

    lo = _lane((A_CHAIN_ROWS, LANES)) < HEAD_DIM
    gq = gq_ref[...]
    gk = gk_ref[...]
    bound_sq = (HEAD_DIM * Q_SCALE_LOG2) ** 2 * jnp.max(gq * gq) * jnp.max(gk * gk)
    bounded = bound_sq <= A_SAFE_SCORE * A_SAFE_SCORE

    n_chunks = tq // A_CHAIN_ROWS
    b16_res = q16_ref.shape[0]

    def dilated_share(ci, of):
        _b16_residues(range(ci * b16_res // of, (ci + 1) * b16_res // of),
                      q16_ref, k16_ref, v16_ref, c16_ref, s16_ref, o16_ref, l16_ref)

    def a_chunk(rows, subtract_max):
        v = v_ref[...]
        cq = cq_ref[rows, :]
        sq = sq_ref[rows, :]
        for j in range(A_Q_W // LANES):
            cols = slice(j * LANES, (j + 1) * LANES)
            q = _rope_slab(_head_rms_slab(q_ref[rows, cols].astype(_F32), gq), cq, sq, 16) * Q_SCALE_LOG2
            outs = []
            for qm in (jnp.where(lo, q, 0.0), jnp.where(lo, 0.0, q)):
                s = jnp.dot(qm.astype(_BF), kt_ref[...], preferred_element_type=_F32)
                if subtract_max:
                    s = s - jnp.max(s, axis=-1, keepdims=True)
                p = jnp.exp2(s)
                l = jnp.sum(p, axis=-1, keepdims=True)
                outs.append(jnp.dot(p.astype(_BF), v, preferred_element_type=_F32) / l)
            o_ref[rows, cols] = jnp.where(lo, outs[0], outs[1]).astype(o_ref.dtype)

    @pl.when(bounded)
    def _():
        for ci in range(n_chunks):
            a_chunk(slice(ci * A_CHAIN_ROWS, (ci + 1) * A_CHAIN_ROWS), False)
            dilated_share(ci, n_chunks)

    @pl.when(jnp.logical_not(bounded))
    def _():
        dilated_share(0, 1)

        def body(ci, carry):
            a_chunk(pl.ds(pl.multiple_of(ci * A_CHAIN_ROWS, A_CHAIN_ROWS), A_CHAIN_ROWS), True)
            return carry

        lax.fori_loop(0, n_chunks, body, 0)


A_STEPS = 4


def _attn_a(nat, perm, cos, sin, cos_p, sin_p, gq, gk):
    b, s, _ = nat.shape
    m16 = perm.shape[2]
    tq = s // A_STEPS
    res = PERM // A_STEPS
    b16 = lambda cb: pl.BlockSpec((None, res, m16, B_W), lambda bi, i: (bi, i, 0, cb))
    t16 = pl.BlockSpec((res, m16, LANES), lambda bi, i: (i, 0, 0))
    return pl.pallas_call(
        functools.partial(_attn_a_kernel, tq=tq),
        grid=(b, A_STEPS),
        in_specs=[
            pl.BlockSpec((None, tq, A_Q_W), lambda bi, i: (bi, i, 0)),
            pl.BlockSpec((None, s, LANES), lambda bi, i: (bi, 0, A_KV_BLOCK)),
            pl.BlockSpec((None, s, LANES), lambda bi, i: (bi, 0, A_KV_BLOCK + 1)),
            pl.BlockSpec((tq, LANES), lambda bi, i: (i, 0)),
            pl.BlockSpec((tq, LANES), lambda bi, i: (i, 0)),
            _const_spec((s, LANES)),
            _const_spec((s, LANES)),
            _const_spec((1, LANES)),
            _const_spec((1, LANES)),
            b16(3), b16(4), b16(5), t16, t16,
        ],
        out_specs=[pl.BlockSpec((None, tq, A_Q_W), lambda bi, i: (bi, i, 0)), b16(0), b16(0)],
        out_shape=[
            jax.ShapeDtypeStruct((b, s, A_Q_W), _BF),
            jax.ShapeDtypeStruct((b, PERM, m16, B_W), _BF),
            jax.ShapeDtypeStruct((b, PERM, m16, B_W), _F32),
        ],
        scratch_shapes=[pltpu.VMEM((LANES, s), _BF)],
        compiler_params=_cparams("parallel", "arbitrary"),
        name="attn_a",
    )(nat, nat, nat, cos, sin, cos, sin, gq, gk, perm, perm, perm, cos_p, sin_p)


def _band_slab(q, kw, vw, valid):
    tq = q.shape[0]
    lo = _lane(q.shape) < HEAD_DIM
    qm = jnp.concatenate([jnp.where(lo, q, 0.0), jnp.where(lo, 0.0, q)], axis=0).astype(_BF)
    s = lax.dot_general(qm, kw, (((1,), (1,)), ((), ())), preferred_element_type=_F32)
    s = jnp.where(jnp.concatenate([valid, valid], axis=0), s, NEG_INF)
    mx = jnp.max(s, axis=-1, keepdims=True)
    pr = jnp.exp2(s - mx)
    l = jnp.sum(pr, axis=-1, keepdims=True)
    o = jnp.dot(pr.astype(_BF), vw, preferred_element_type=_F32) / l
    lse = (mx + jnp.log2(l)) * LN2
    return jnp.where(lo, o[:tq], o[tq:]), jnp.where(lo, lse[:tq], lse[tq:])


def _attn_b1_kernel(q_ref, k_ref, v_ref, c_ref, s_ref, o_ref, lse_ref, kr_ref, *, m_len, tq, win):
    half = ROPE_DIMS // 2
    for p in range(N_SLAB):
        sl = slice(p * LANES, (p + 1) * LANES)
        kr_ref[:, sl] = _rope_slab(k_ref[:, sl].astype(_F32), c_ref[...], s_ref[...], half).astype(_BF)

    row = lax.broadcasted_iota(jnp.int32, (tq, win), 0)
    col = lax.broadcasted_iota(jnp.int32, (tq, win), 1)

    def block(i, carry):
        q0 = pl.multiple_of(i * tq, tq)
        start = pl.multiple_of(jnp.clip(q0 - B_RADIUS, 0, m_len - win), B_RADIUS)
        valid = jnp.abs(row - col + (q0 - start)) <= B_RADIUS
        cq = c_ref[pl.ds(q0, tq), :]
        sq = s_ref[pl.ds(q0, tq), :]
        for p in range(N_SLAB):
            sl = slice(p * LANES, (p + 1) * LANES)
            q = _rope_slab(q_ref[pl.ds(q0, tq), sl].astype(_F32), cq, sq, half) * Q_SCALE_LOG2
            o, lse = _band_slab(q, kr_ref[pl.ds(start, win), sl], v_ref[pl.ds(start, win), sl], valid)
            o_ref[pl.ds(q0, tq), sl] = o.astype(o_ref.dtype)
            lse_ref[pl.ds(q0, tq), sl] = lse
        return carry

    lax.fori_loop(0, m_len // tq, block, 0, unroll=4)


def _attn_b1(nat, cos, sin):
    b, s, _ = nat.shape
    tq = 128
    win = min(tq + 2 * B_RADIUS, s)
    c0 = A_Q_W // B_W
    o, lse = pl.pallas_call(
        functools.partial(_attn_b1_kernel, m_len=s, tq=tq, win=win),
        grid=(b,),
        in_specs=[
            pl.BlockSpec((None, s, B_W), lambda bi: (bi, 0, c0)),
            pl.BlockSpec((None, s, B_W), lambda bi: (bi, 0, c0 + 1)),
            pl.BlockSpec((None, s, B_W), lambda bi: (bi, 0, c0 + 2)),
            _const_spec((s, LANES)),
            _const_spec((s, LANES)),
        ],
        out_specs=[pl.BlockSpec((None, s, B_W), lambda bi: (bi, 0, 0))] * 2,
        out_shape=[jax.ShapeDtypeStruct((b, s, B_W), _BF), jax.ShapeDtypeStruct((b, s, B_W), _F32)],
        scratch_shapes=[pltpu.VMEM((s, B_W), _BF)],
        compiler_params=_cparams("parallel"),
        name="attn_b1",
    )(nat, nat, nat, cos, sin)
    return o.reshape(b * s, B_W), lse.reshape(b * s, B_W)


B4_SEG = 4
B4_TQS = 32


def _b4_rope_keys(k_ref, c_ref, s_ref, kr_ref):
    half = ROPE_DIMS // 2
    for j in range(B4_SEG):
        for p in range(N_SLAB):
            sl = slice(p * LANES, (p + 1) * LANES)
            kr_ref[j, :, sl] = _rope_slab(k_ref[j, :, sl].astype(_F32), c_ref[j], s_ref[j], half).astype(_BF)


def _b4_blocks(blocks, q_ref, kr_ref, v_ref, c_ref, s_ref, o_ref, lse_ref):
    half = ROPE_DIMS // 2
    seg, tqs = B4_SEG, B4_TQS
    m16 = q_ref.shape[1]
    wks = min(tqs + 2 * (B_RADIUS // seg), m16)
    tq, win = seg * tqs, seg * wks
    row = lax.broadcasted_iota(jnp.int32, (tq, win), 0)
    col = lax.broadcasted_iota(jnp.int32, (tq, win), 1)
    for i in blocks:
        a_q = i * tqs
        a_k = min(max(a_q - B_RADIUS // seg, 0), m16 - wks)
        qpos = seg * (a_q + (row & (tqs - 1))) + (row >> (tqs.bit_length() - 1))
        kpos = seg * (a_k + (col & (wks - 1))) + (col >> (wks.bit_length() - 1))
        valid = jnp.abs(qpos - kpos) <= B_RADIUS
        cq = jnp.concatenate([c_ref[j, a_q:a_q + tqs, :] for j in range(seg)], axis=0)
        sq = jnp.concatenate([s_ref[j, a_q:a_q + tqs, :] for j in range(seg)], axis=0)
        for p in range(N_SLAB):
            sl = slice(p * LANES, (p + 1) * LANES)
            q = jnp.concatenate([q_ref[j, a_q:a_q + tqs, sl] for j in range(seg)], axis=0).astype(_F32)
            q = _rope_slab(q, cq, sq, half) * Q_SCALE_LOG2
            kw = jnp.concatenate([kr_ref[j, a_k:a_k + wks, sl] for j in range(seg)], axis=0)
            vw = jnp.concatenate([v_ref[j, a_k:a_k + wks, sl] for j in range(seg)], axis=0)
            o, lse = _band_slab(q, kw, vw, valid)
            for j in range(seg):
                o_ref[j, a_q:a_q + tqs, sl] = o[j * tqs:(j + 1) * tqs].astype(o_ref.dtype)
                lse_ref[j, a_q:a_q + tqs, sl] = lse[j * tqs:(j + 1) * tqs]


def _attn_b4_kernel(q_ref, k_ref, v_ref, c_ref, s_ref, o_ref, lse_ref, kr_ref):
    _b4_rope_keys(k_ref, c_ref, s_ref, kr_ref)
    _b4_blocks(range(q_ref.shape[1] // B4_TQS), q_ref, kr_ref, v_ref, c_ref, s_ref, o_ref, lse_ref)


def _attn_b4(perm, cos_p, sin_p):
    b, _, m16, cols = perm.shape
    pv = perm.reshape(b, B4_SEG, 4, m16, cols)
    cv = cos_p.reshape(B4_SEG, 4, m16, LANES)
    sv = sin_p.reshape(B4_SEG, 4, m16, LANES)
    blk = lambda cb: pl.BlockSpec((None, B4_SEG, None, m16, B_W), lambda bi, r: (bi, 0, r, 0, cb))
    tab = pl.BlockSpec((B4_SEG, None, m16, LANES), lambda bi, r: (0, r, 0, 0))
    o, lse = pl.pallas_call(
        _attn_b4_kernel,
        grid=(b, 4),
        in_specs=[blk(0), blk(1), blk(2), tab, tab],
        out_specs=[blk(0), blk(0)],
        out_shape=[jax.ShapeDtypeStruct((b, B4_SEG, 4, m16, B_W), _BF), jax.ShapeDtypeStruct((b, B4_SEG, 4, m16, B_W), _F32)],
        scratch_shapes=[pltpu.VMEM((B4_SEG, m16, B_W), _BF)],
        compiler_params=_cparams("parallel", "parallel"),
        name="attn_b4",
    )(pv, pv, pv, cv, sv)
    return o.reshape(b, PERM, m16, B_W), lse.reshape(b, PERM, m16, B_W)


def _b16_residues(residues, q_ref, k_ref, v_ref, c_ref, s_ref, o_ref, lse_ref):
    half = ROPE_DIMS // 2
    m16 = q_ref.shape[1]
    tq = min(128, m16)
    win = min(tq + 2 * B_RADIUS, m16)
    row = lax.broadcasted_iota(jnp.int32, (tq, win), 0)
    col = lax.broadcasted_iota(jnp.int32, (tq, win), 1)
    for r in residues:
        for i in range(m16 // tq):
            q0 = i * tq
            start = min(max(q0 - B_RADIUS, 0), m16 - win)
            valid = jnp.abs(row - col + (q0 - start)) <= B_RADIUS
            for p in range(N_SLAB):
                sl = slice(p * LANES, (p + 1) * LANES)
                q = _rope_slab(q_ref[r, q0:q0 + tq, sl].astype(_F32), c_ref[r, q0:q0 + tq, :], s_ref[r, q0:q0 + tq, :], half)
                kw = _rope_slab(k_ref[r, start:start + win, sl].astype(_F32), c_ref[r, start:start + win, :],
                                s_ref[r, start:start + win, :], half).astype(_BF)
                o, lse = _band_slab(q * Q_SCALE_LOG2, kw, v_ref[r, start:start + win, sl], valid)
                o_ref[r, q0:q0 + tq, sl] = o.astype(o_ref.dtype)
                lse_ref[r, q0:q0 + tq, sl] = lse


def _even_out_kernel(x_ref, a_ref, o0_ref, l0_ref, o1_ref, l1_ref, o2_ref, l2_ref, w_ref, y_ref):
    l1, l2 = l1_ref[...], l2_ref[...]
    m12 = jnp.maximum(l1, l2)
    e1, e2 = jnp.exp(l1 - m12), jnp.exp(l2 - m12)
    den12 = e1 + e2
    o12 = _from_residue_major((e1 * o1_ref[...].astype(_F32) + e2 * o2_ref[...].astype(_F32)) / den12)
    l12 = _from_residue_major(m12 + jnp.log(den12))
    l0 = l0_ref[...]
    mx = jnp.maximum(l0, l12)
    e0, e12 = jnp.exp(l0 - mx), jnp.exp(l12 - mx)
    bo = (e0 * o0_ref[...].astype(_F32) + e12 * o12) / (e0 + e12)
    y = jnp.dot(a_ref[...], w_ref[:A_Q_W, :], preferred_element_type=_F32)
    y = y + jnp.dot(bo.astype(_BF), w_ref[A_Q_W:, :], preferred_element_type=_F32)
    y_ref[...] = x_ref[...] + y


def _even_out(x, a_o, o0, l0, o1, l1, o2, l2, w, *, s, tm):
    n, d = x.shape
    tpb = s // tm
    row = lambda width: pl.BlockSpec((tm, width), lambda i: (i, 0))
    prm = pl.BlockSpec((None, PERM, tm // PERM, B_W), lambda i: (i // tpb, 0, i % tpb, 0))
    return pl.pallas_call(
        _even_out_kernel,
        grid=(n // tm,),
        in_specs=[row(d), row(A_Q_W), row(B_W), row(B_W), prm, prm, prm, prm, _const_spec(w.shape)],
        out_specs=row(d),
        out_shape=jax.ShapeDtypeStruct((n, d), _F32),
        compiler_params=_cparams("parallel"),
        name="even_out",
    )(x, a_o, o0, l0, o1, l1, o2, l2, w)


def _ffn_kernel(*refs, chunk, final_norm, mixer_proj):
    if mixer_proj:
        x_ref, hs_ref, gate_ref, wo_ref, g_ref, wg_ref, wu_ref, wd_ref, gf_ref, y_ref, h_ref, act_ref = refs
        rows = h_ref.shape[0]
        hs = jnp.swapaxes(hs_ref[...].astype(_F32), 0, 1).reshape(rows, D_RNN)
        a = (hs * gate_ref[...].reshape(rows, D_RNN).astype(_F32)).astype(_BF)
        x = x_ref[...].reshape(rows, D_MODEL) + jnp.dot(a, wo_ref[...], preferred_element_type=_F32)
    else:
        x_ref, g_ref, wg_ref, wu_ref, wd_ref, gf_ref, y_ref, h_ref, act_ref = refs
        x = x_ref[...]
    h_ref[...] = _rms_rows(x, g_ref[...]).astype(_BF)
    for c in range(D_FF // chunk):
        sl = slice(c * chunk, (c + 1) * chunk)
        gate = jnp.dot(h_ref[...], wg_ref[:, sl], preferred_element_type=_F32)
        up = jnp.dot(h_ref[...], wu_ref[:, sl], preferred_element_type=_F32)
        act_ref[:, sl] = (gate * _sigmoid(gate) * up).astype(_BF)
    y = x + jnp.dot(act_ref[...], wd_ref[...], preferred_element_type=_F32)
    if final_norm:
        y = _rms_rows(y, gf_ref[...])
    y_ref[...] = y.reshape(y_ref.shape)


def _ffn(x, g, w_gate_up, w_down, g_final, *, tm, final_norm, hsum_tm=None, gate=None, w_mixer=None):
    b, s, d = x.shape
    mixer_proj = hsum_tm is not None
    const = lambda shape, *idx: pl.BlockSpec(shape, lambda *_: idx, pipeline_mode=pl.Buffered(1))
    weights = [const((1, d), 0, 0), const((d, D_FF), 0, 0), const((d, D_FF), 0, 1), const((D_FF, d), 0, 0), const((1, d), 0, 0)]
    w_args = [g, w_gate_up, w_gate_up, w_down, g_final]
    if mixer_proj:
        tt = tm // SCAN_BATCH
        grid = (b // SCAN_BATCH, s // tt)
        xs = pl.BlockSpec((SCAN_BATCH, tt, d), lambda gi, c: (gi, c, 0))
        in_specs = [xs, pl.BlockSpec((None, tt, SCAN_BATCH, d), lambda gi, c: (gi, c, 0, 0)), xs, const(w_mixer.shape, 0, 0)] + weights
        args = [x, hsum_tm, gate, w_mixer] + w_args
        sem = ("parallel", "parallel")
    else:
        x = x.reshape(b * s, d)
        grid = (b * s // tm,)
        xs = pl.BlockSpec((tm, d), lambda i: (i, 0))
        in_specs = [xs] + weights
        args = [x] + w_args
        sem = ("parallel",)
    y = pl.pallas_call(
        functools.partial(_ffn_kernel, chunk=256, final_norm=final_norm, mixer_proj=mixer_proj),
        grid=grid,
        in_specs=in_specs,
        out_specs=xs,
        out_shape=jax.ShapeDtypeStruct(x.shape, _F32),
        scratch_shapes=[pltpu.VMEM((tm, d), _BF), pltpu.VMEM((tm, D_FF), _BF)],
        compiler_params=_cparams(*sem),
        name="ffn_mix" if mixer_proj else "ffn",
    )(*args)
    return y.reshape(b, s, d)


SCAN_BATCH = 8
SCAN_UNROLL = 4
HALO = 8


def _conv_kernel(u_ref, prev_ref, next_ref, w_ref, b_ref, o_ref, *, tc):
    c = pl.program_id(1)
    has_prev = (c > 0).astype(_F32)
    has_next = (c < pl.num_programs(1) - 1).astype(_F32)
    t = lax.broadcasted_iota(jnp.int32, (tc, LANES), 0)
    for cb in range(D_RNN // LANES):
        sl = slice(cb * LANES, (cb + 1) * LANES)
        w = w_ref[:, sl]
        bias = b_ref[:, sl]
        rows = []
        for bi in range(SCAN_BATCH):
            u = u_ref[bi, :, sl]
            p_last = prev_ref[bi, HALO - 1:HALO, sl] * has_prev
            n0 = next_ref[bi, 0:1, sl] * has_next
            n1 = next_ref[bi, 1:2, sl] * has_next
            um1 = jnp.where(t == 0, p_last, pltpu.roll(u, 1, 0))
            up1 = jnp.where(t == tc - 1, n0, pltpu.roll(u, tc - 1, 0))
            up2 = jnp.where(t == tc - 2, n0, jnp.where(t == tc - 1, n1, pltpu.roll(u, tc - 2, 0)))
            rows.append(um1 * w[0:1] + u * w[1:2] + up1 * w[2:3] + up2 * w[3:4] + bias)
        o_ref[:, :, sl] = jnp.swapaxes(jnp.stack(rows, axis=0), 0, 1)


def _conv(u, w, bias, *, tc):
    b, s, d = u.shape
    last = s // HALO - 1
    per = tc // HALO
    return pl.pallas_call(
        functools.partial(_conv_kernel, tc=tc),
        grid=(b // SCAN_BATCH, s // tc),
        in_specs=[
            pl.BlockSpec((SCAN_BATCH, tc, d), lambda g, c: (g, c, 0)),
            pl.BlockSpec((SCAN_BATCH, HALO, d), lambda g, c: (g, jnp.maximum(c * per - 1, 0), 0)),
            pl.BlockSpec((SCAN_BATCH, HALO, d), lambda g, c: (g, jnp.minimum((c + 1) * per, last), 0)),
            _const_spec(w.shape),
            _const_spec((1, d)),
        ],
        out_specs=pl.BlockSpec((None, tc, SCAN_BATCH, d), lambda g, c: (g, c, 0, 0)),
        out_shape=jax.ShapeDtypeStruct((b // SCAN_BATCH, s, SCAN_BATCH, d), _F32),
        compiler_params=_cparams("parallel", "parallel"),
        name="conv",
    )(u, u, u, w, bias)


def _scan_kernel(*refs, tc, reverse, finalize):
    if finalize:
        u_ref, w_ref, ba_ref, bx_ref, lam_ref, hf_ref, o_ref, a_s, b_s, h_s, carry = refs
    else:
        u_ref, w_ref, ba_ref, bx_ref, lam_ref, o_ref, a_s, b_s, h_s, carry = refs
    rows = SCAN_BATCH * tc

    @pl.when(pl.program_id(1) == 0)
    def _():
        carry[...] = jnp.zeros_like(carry)

    for n in range(LRU_BLOCKS):
        sl = slice(n * LRU_BW, (n + 1) * LRU_BW)
        lam = lam_ref[:, sl]
        c2 = (-4.0 * LOG2E) * (jnp.maximum(-lam, 0.0) + jnp.log1p(jnp.exp(-jnp.abs(lam))))
        xh = u_ref[:, :, sl].reshape(rows, LRU_BW)
        z = jnp.dot(xh.astype(_BF), w_ref[n], preferred_element_type=_F32)
        th_a = jnp.tanh(z[:, :LRU_BW] + ba_ref[:, sl])
        th_x = jnp.tanh(z[:, LRU_BW:] + bx_ref[:, sl])
        a = jnp.exp2(c2 + c2 * th_a)
        a_s[n] = a
        v = 1.0 - a * a
        root = jnp.where(v > 0.0, v * lax.rsqrt(v), 0.0)
        b_s[n] = root * ((1.0 + th_x) * xh)

    def step(i, hs):
        t = (tc - 1 - i) if reverse else i
        idx = pl.ds(pl.multiple_of(t * SCAN_BATCH, SCAN_BATCH), SCAN_BATCH)
        new = []
        for n in range(LRU_BLOCKS):
            h = a_s[n, idx, :] * hs[n] + b_s[n, idx, :]
            h_s[n, idx, :] = h
            new.append(h)
        return tuple(new)

    hs = lax.fori_loop(0, tc, step, tuple(carry[n] for n in range(LRU_BLOCKS)), unroll=SCAN_UNROLL)
    for n in range(LRU_BLOCKS):
        carry[n] = hs[n]

    for n in range(LRU_BLOCKS):
        sl = slice(n * LRU_BW, (n + 1) * LRU_BW)
        h = h_s[n].reshape(tc, SCAN_BATCH, LRU_BW)
        if finalize:
            o_ref[:, :, sl] = (hf_ref[:, :, sl] + h).astype(o_ref.dtype)
        else:
            o_ref[:, :, sl] = h


def _scan(u, w, ba, bx, lam, *, tc, reverse, h_fwd=None):
    g8, s, _, d = u.shape
    nt = s // tc
    finalize = h_fwd is not None
    tmap = (lambda g, c: (g, nt - 1 - c, 0, 0)) if reverse else (lambda g, c: (g, c, 0, 0))
    blk = pl.BlockSpec((None, tc, SCAN_BATCH, d), tmap)
    in_specs = [blk, _const_spec(w.shape), _const_spec((1, d)), _const_spec((1, d)), _const_spec((1, d))]
    args = [u, w, ba, bx, lam]
    if finalize:
        in_specs += [blk]
        args += [h_fwd]
    work = pltpu.VMEM((LRU_BLOCKS, SCAN_BATCH * tc, LRU_BW), _F32)
    return pl.pallas_call(
        functools.partial(_scan_kernel, tc=tc, reverse=reverse, finalize=finalize),
        grid=(g8, nt),
        in_specs=in_specs,
        out_specs=blk,
        out_shape=jax.ShapeDtypeStruct(u.shape, _BF if finalize else _F32),
        scratch_shapes=[work, work, work, pltpu.VMEM((LRU_BLOCKS, SCAN_BATCH, LRU_BW), _F32)],
        compiler_params=_cparams("parallel", "arbitrary"),
        name="scan_bwd" if reverse else "scan_fwd",
    )(*args)


def _rope_tables(pos_lo, pos_hi, n_rot, theta, s):
    half = n_rot // 2
    inv = jnp.power(jnp.float32(theta), -jnp.arange(half, dtype=_F32) * (2.0 / n_rot))

    def part(pos):
        ang = pos.astype(_F32)[:, None] * inv[None, :]
        c, sn = jnp.cos(ang), jnp.sin(ang)
        return jnp.concatenate([c, c], axis=-1), jnp.concatenate([-sn, sn], axis=-1)

    cs, sns = [], []
    for pos in (pos_lo, pos_hi):
        if pos is not None:
            c, sn = part(pos)
            cs.append(c)
            sns.append(sn)
    width = sum(c.shape[-1] for c in cs)
    cs.append(jnp.ones((s, HEAD_DIM - width), _F32))
    sns.append(jnp.zeros((s, HEAD_DIM - width), _F32))
    c = jnp.concatenate(cs, axis=-1)
    sn = jnp.concatenate(sns, axis=-1)
    return jnp.concatenate([c, c], axis=-1), jnp.concatenate([sn, sn], axis=-1)


def _even_w_in_layout(w):
    d = w.shape[0]
    aq = w[:, :A_Q_W].reshape(d, A_Q_HEADS, HEAD_DIM)[:, jnp.array(A_HEAD_ORDER), :].reshape(d, A_Q_W)
    akv = w[:, A_Q_W:A_Q_W + 2 * LANES]
    rest = w[:, A_Q_W + 2 * LANES:]
    w_nat = jnp.concatenate([aq, rest[:, :3 * B_W], akv], axis=1).astype(_BF)
    return w_nat, rest[:, 3 * B_W:].astype(_BF)


def _even_w_out_layout(w):
    d = w.shape[1]
    wa = w[:A_Q_W].reshape(A_Q_HEADS, HEAD_DIM, d)[jnp.array(A_HEAD_ORDER)].reshape(A_Q_W, d)
    return jnp.concatenate([wa, w[A_Q_W:]], axis=0).astype(_BF)


def kernel(x, mix_norm, ffn_norm, final_norm, even_w_in, even_q_norm, even_k_norm, even_w_out, odd_w_in, odd_conv_w, odd_conv_b, odd_gate_a_w, odd_gate_a_b, odd_gate_x_w, odd_gate_x_b, odd_lambda, odd_w_out, ffn_w_gate_up, ffn_w_down):
    b, s, d = x.shape
    n = b * s
    depth = mix_norm.shape[0]
    m16 = s // PERM
    assert d == D_MODEL and s % 512 == 0 and b % SCAN_BATCH == 0

    pos = jnp.arange(s, dtype=jnp.int32)
    cos_a, sin_a = _rope_tables(pos // GRID_W, pos % GRID_W, HEAD_DIM // 2, AXIAL_THETA, s)
    cos_b, sin_b = _rope_tables(pos, None, ROPE_DIMS, ROPE_THETA, s)
    cos_p = cos_b.reshape(m16, PERM, LANES).transpose(1, 0, 2)
    sin_p = sin_b.reshape(m16, PERM, LANES).transpose(1, 0, 2)

    tm = 512
    xf = x.reshape(n, d)
    for layer in range(depth):
        j = layer // 2
        g_mix = mix_norm[layer].reshape(1, d)
        if layer % 2 == 0:
            w_nat, w_perm = _even_w_in_layout(even_w_in[j])
            nat, perm = _even_in(xf, g_mix, w_nat, w_perm, b=b, s=s, tm=tm)
            nat3 = nat.reshape(b, s, NAT_COLS)
            gq = jnp.tile(even_q_norm[j], 2).reshape(1, LANES)
            gk = jnp.tile(even_k_norm[j], 2).reshape(1, LANES)
            a_o, o2, l2 = _attn_a(nat3, perm, cos_a, sin_a, cos_p, sin_p, gq, gk)
            o0, l0 = _attn_b1(nat3, cos_b, sin_b)
            o1, l1 = _attn_b4(perm, cos_p, sin_p)
            xf = _even_out(xf, a_o.reshape(n, A_Q_W), o0, l0, o1, l1, o2, l2, _even_w_out_layout(even_w_out[j]), s=s, tm=tm)
            mix = {}
        else:
            gate, u = _odd_in(xf, g_mix, odd_w_in[j].astype(_BF), tm=tm)
            uc = _conv(u.reshape(b, s, D_RNN), 0.5 * odd_conv_w[j], 0.5 * odd_conv_b[j].reshape(1, D_RNN), tc=128)
            h = None
            for direction in range(2):
                w = jnp.concatenate([odd_gate_a_w[j, direction], odd_gate_x_w[j, direction]], axis=-1).astype(_BF)
                h = _scan(uc, w, 0.5 * odd_gate_a_b[j, direction].reshape(1, D_RNN), 0.5 * odd_gate_x_b[j, direction].reshape(1, D_RNN),
                          odd_lambda[j, direction].reshape(1, D_RNN), tc=128, reverse=direction == 1, h_fwd=h)
            mix = dict(hsum_tm=h, gate=gate.reshape(b, s, D_RNN), w_mixer=odd_w_out[j].astype(_BF))
        xf = _ffn(xf.reshape(b, s, d), ffn_norm[layer].reshape(1, d), ffn_w_gate_up[layer].astype(_BF), ffn_w_down[layer].astype(_BF),
                  final_norm.reshape(1, d), tm=tm, final_norm=layer == depth - 1, **mix).reshape(n, d)
    return xf.reshape(b, s, d)
```

```python
import functools

import jax
import jax.numpy as jnp
from jax import lax
from jax.experimental import pallas as pl
from jax.experimental.pallas import tpu as pltpu

D_MODEL = 1024
HEAD_DIM = 64
A_Q_HEADS = 8
A_KV_HEADS = 2
B_HEADS = 8
B_RADIUS = 64
GRID_W = 64
AXIAL_THETA = 10000.0
ROPE_THETA = 500000.0
ROPE_DIMS = HEAD_DIM // 4
D_RNN = D_MODEL
LRU_BLOCKS = 8
LRU_BW = D_RNN // LRU_BLOCKS
LRU_C = 8.0
D_FF = 2816
EPS = 1e-6
NEG_INF = -1e30
LOG2E = 1.4426950408889634
LN2 = 0.6931471805599453
Q_SCALE_LOG2 = HEAD_DIM ** -0.5 * LOG2E

LANES = 128
A_Q_W = A_Q_HEADS * HEAD_DIM
B_W = B_HEADS * HEAD_DIM
N_SLAB = B_W // LANES
A_CHAIN_ROWS = 256
A_SAFE_SCORE = 50.0
A_HEAD_ORDER = (0, 4, 1, 5, 2, 6, 3, 7)
NAT_COLS = A_Q_W + 3 * B_W + 2 * LANES
A_KV_BLOCK = (A_Q_W + 3 * B_W) // LANES
PERM = 16
PROJ_CHUNK = 512
ODD_CHUNK = 256

VMEM_LIMIT = 56 * 1024 * 1024

_BF = jnp.bfloat16
_F32 = jnp.float32


def _cparams(*sem):
    return pltpu.CompilerParams(dimension_semantics=sem, vmem_limit_bytes=VMEM_LIMIT)


def _const_spec(shape):
    return pl.BlockSpec(shape, lambda *_: (0,) * len(shape), pipeline_mode=pl.Buffered(1))


def _rms_rows(x, g):
    ms = jnp.mean(x * x, axis=-1, keepdims=True)
    return x * lax.rsqrt(ms + EPS) * g


def _sigmoid(x):
    return 0.5 * (1.0 + jnp.tanh(0.5 * x))


def _lane(shape):
    return lax.broadcasted_iota(jnp.int32, shape, len(shape) - 1)


def _to_residue_major(y):
    rows, c = y.shape
    return jnp.swapaxes(y.reshape(rows // PERM, PERM, c), 0, 1).reshape(rows, c)


def _from_residue_major(y):
    p, m, c = y.shape
    return jnp.swapaxes(y, 0, 1).reshape(p * m, c)


def _even_in_kernel(x_ref, g_ref, wn_ref, wp_ref, nat_ref, perm_ref, hn_ref, hp_ref):
    cn = wn_ref.shape[1]
    hr = x_ref.shape[0] // 2
    pr = hr // PERM
    for h in range(2):
        rows = slice(h * hr, (h + 1) * hr)
        y = _rms_rows(x_ref[rows, :], g_ref[...])
        hn_ref[rows, :] = y.astype(_BF)
        hp_ref[rows, :] = _to_residue_major(y).astype(_BF)
        for c0 in range(0, cn, PROJ_CHUNK):
            sl = slice(c0, min(c0 + PROJ_CHUNK, cn))
            nat_ref[rows, sl] = jnp.dot(hn_ref[rows, :], wn_ref[:, sl], preferred_element_type=_F32).astype(_BF)
        for c in range(wp_ref.shape[1] // PROJ_CHUNK):
            sl = slice(c * PROJ_CHUNK, (c + 1) * PROJ_CHUNK)
            acc = jnp.dot(hp_ref[rows, :], wp_ref[:, sl], preferred_element_type=_F32)
            perm_ref[:, h * pr:(h + 1) * pr, sl] = acc.reshape(PERM, pr, PROJ_CHUNK).astype(_BF)


def _even_in(x, g, w_nat, w_perm, *, b, s, tm):
    n, d = x.shape
    tpb = s // tm
    cn, cp = w_nat.shape[1], w_perm.shape[1]
    return pl.pallas_call(
        _even_in_kernel,
        grid=(n // tm,),
        in_specs=[pl.BlockSpec((tm, d), lambda i: (i, 0)), _const_spec((1, d)), _const_spec((d, cn)), _const_spec((d, cp))],
        out_specs=[
            pl.BlockSpec((tm, cn), lambda i: (i, 0)),
            pl.BlockSpec((None, PERM, tm // PERM, cp), lambda i: (i // tpb, 0, i % tpb, 0)),
        ],
        out_shape=[
            jax.ShapeDtypeStruct((n, cn), _BF),
            jax.ShapeDtypeStruct((b, PERM, s // PERM, cp), _BF),
        ],
        scratch_shapes=[pltpu.VMEM((tm, d), _BF), pltpu.VMEM((tm, d), _BF)],
        compiler_params=_cparams("parallel"),
        name="even_in",
    )(x, g, w_nat, w_perm)


def _gelu_tanh(x):
    return 0.5 * x * (1.0 + jnp.tanh(0.7978845608028654 * (x + 0.044715 * (x * x * x))))


def _odd_in_kernel(x_ref, g_ref, w_ref, gate_ref, u_ref, h_ref):
    h_ref[...] = _rms_rows(x_ref[...], g_ref[...]).astype(_BF)
    for c in range(D_RNN // ODD_CHUNK):
        sl = slice(c * ODD_CHUNK, (c + 1) * ODD_CHUNK)
        u_ref[:, sl] = jnp.dot(h_ref[...], w_ref[:, D_RNN + c * ODD_CHUNK:D_RNN + (c + 1) * ODD_CHUNK],
                               preferred_element_type=_F32)
        acc = jnp.dot(h_ref[...], w_ref[:, sl], preferred_element_type=_F32)
        gate_ref[:, sl] = _gelu_tanh(acc).astype(gate_ref.dtype)


def _odd_in(x, g, w, *, tm):
    n, d = x.shape
    return pl.pallas_call(
        _odd_in_kernel,
        grid=(n // tm,),
        in_specs=[pl.BlockSpec((tm, d), lambda i: (i, 0)), _const_spec((1, d)), _const_spec(w.shape)],
        out_specs=[pl.BlockSpec((tm, D_RNN), lambda i: (i, 0)), pl.BlockSpec((tm, D_RNN), lambda i: (i, 0))],
        out_shape=[jax.ShapeDtypeStruct((n, D_RNN), _BF), jax.ShapeDtypeStruct((n, D_RNN), _F32)],
        scratch_shapes=[pltpu.VMEM((tm, d), _BF)],
        compiler_params=_cparams("parallel"),
        name="odd_in",
    )(x, g, w)


def _rope_slab(x, cos, sin_signed, half):
    lane = _lane(x.shape)
    fwd = pltpu.roll(x, LANES - half, 1)
    bwd = pltpu.roll(x, half, 1)
    partner = jnp.where((lane % (2 * half)) < half, fwd, bwd)
    return x * cos + partner * sin_signed


def _head_rms_slab(x, g):
    lo = _lane(x.shape) < HEAD_DIM
    sq = x * x
    ms_lo = jnp.sum(jnp.where(lo, sq, 0.0), axis=-1, keepdims=True) * (1.0 / HEAD_DIM)
    ms_hi = jnp.sum(jnp.where(lo, 0.0, sq), axis=-1, keepdims=True) * (1.0 / HEAD_DIM)
    rs = jnp.where(lo, lax.rsqrt(ms_lo + EPS), lax.rsqrt(ms_hi + EPS))
    return x * rs * g


def _attn_a_kernel(q_ref, k_ref, v_ref, cq_ref, sq_ref, ck_ref, sk_ref, gq_ref, gk_ref,
                   q16_ref, k16_ref, v16_ref, c16_ref, s16_ref, o_ref, o16_ref, l16_ref, kt_ref, *, tq):
    @pl.when(pl.program_id(1) == 0)
    def _():
        k = _head_rms_slab(k_ref[...].astype(_F32), gk_ref[...])
        k = _rope_slab(k, ck_ref[...], sk_ref[...], 16)
        kt_ref[...] = k.T.astype(_BF)

    lo = _lane((A_CHAIN_ROWS, LANES)) < HEAD_DIM
    gq = gq_ref[...]
    gk = gk_ref[...]
    bound_sq = (HEAD_DIM * Q_SCALE_LOG2) ** 2 * jnp.max(gq * gq) * jnp.max(gk * gk)
    bounded = bound_sq <= A_SAFE_SCORE * A_SAFE_SCORE

    n_chunks = tq // A_CHAIN_ROWS
    b16_res = q16_ref.shape[0]

    def dilated_share(ci, of):
        _b16_residues(range(ci * b16_res // of, (ci + 1) * b16_res // of),
                      q16_ref, k16_ref, v16_ref, c16_ref, s16_ref, o16_ref, l16_ref)

    def a_chunk(rows, subtract_max):
        v = v_ref[...]
        cq = cq_ref[rows, :]
        sq = sq_ref[rows, :]
        for j in range(A_Q_W // LANES):
            cols = slice(j * LANES, (j + 1) * LANES)
            q = _rope_slab(_head_rms_slab(q_ref[rows, cols].astype(_F32), gq), cq, sq, 16) * Q_SCALE_LOG2
            outs = []
            for qm in (jnp.where(lo, q, 0.0), jnp.where(lo, 0.0, q)):
                s = jnp.dot(qm.astype(_BF), kt_ref[...], preferred_element_type=_F32)
                if subtract_max:
                    s = s - jnp.max(s, axis=-1, keepdims=True)
                p = jnp.exp2(s)
                l = jnp.sum(p, axis=-1, keepdims=True)
                outs.append(jnp.dot(p.astype(_BF), v, preferred_element_type=_F32) / l)
            o_ref[rows, cols] = jnp.where(lo, outs[0], outs[1]).astype(o_ref.dtype)

    @pl.when(bounded)
    def _():
        for ci in range(n_chunks):
            a_chunk(slice(ci * A_CHAIN_ROWS, (ci + 1) * A_CHAIN_ROWS), False)
            dilated_share(ci, n_chunks)

    @pl.when(jnp.logical_not(bounded))
    def _():
        dilated_share(0, 1)

        def body(ci, carry):
            a_chunk(pl.ds(pl.multiple_of(ci * A_CHAIN_ROWS, A_CHAIN_ROWS), A_CHAIN_ROWS), True)
            return carry

        lax.fori_loop(0, n_chunks, body, 0)


A_STEPS = 4


def _attn_a(nat, perm, cos, sin, cos_p, sin_p, gq, gk):
    b, s, _ = nat.shape
    m16 = perm.shape[2]
    tq = s // A_STEPS
    res = PERM // A_STEPS
    b16 = lambda cb: pl.BlockSpec((None, res, m16, B_W), lambda bi, i: (bi, i, 0, cb))
    t16 = pl.BlockSpec((res, m16, LANES), lambda bi, i: (i, 0, 0))
    return pl.pallas_call(
        functools.partial(_attn_a_kernel, tq=tq),
        grid=(b, A_STEPS),
        in_specs=[
            pl.BlockSpec((None, tq, A_Q_W), lambda bi, i: (bi, i, 0)),
            pl.BlockSpec((None, s, LANES), lambda bi, i: (bi, 0, A_KV_BLOCK)),
            pl.BlockSpec((None, s, LANES), lambda bi, i: (bi, 0, A_KV_BLOCK + 1)),
            pl.BlockSpec((tq, LANES), lambda bi, i: (i, 0)),
            pl.BlockSpec((tq, LANES), lambda bi, i: (i, 0)),
            _const_spec((s, LANES)),
            _const_spec((s, LANES)),
            _const_spec((1, LANES)),
            _const_spec((1, LANES)),
            b16(3), b16(4), b16(5), t16, t16,
        ],
        out_specs=[pl.BlockSpec((None, tq, A_Q_W), lambda bi, i: (bi, i, 0)), b16(0), b16(0)],
        out_shape=[
            jax.ShapeDtypeStruct((b, s, A_Q_W), _BF),
            jax.ShapeDtypeStruct((b, PERM, m16, B_W), _BF),
            jax.ShapeDtypeStruct((b, PERM, m16, B_W), _F32),
        ],
        scratch_shapes=[pltpu.VMEM((LANES, s), _BF)],
        compiler_params=_cparams("parallel", "arbitrary"),
        name="attn_a",
    )(nat, nat, nat, cos, sin, cos, sin, gq, gk, perm, perm, perm, cos_p, sin_p)


def _band_slab(q, kw, vw, valid):
    tq = q.shape[0]
    lo = _lane(q.shape) < HEAD_DIM
    qm = jnp.concatenate([jnp.where(lo, q, 0.0), jnp.where(lo, 0.0, q)], axis=0).astype(_BF)
    s = lax.dot_general(qm, kw, (((1,), (1,)), ((), ())), preferred_element_type=_F32)
    s = jnp.where(jnp.concatenate([valid, valid], axis=0), s, NEG_INF)
    mx = jnp.max(s, axis=-1, keepdims=True)
    pr = jnp.exp2(s - mx)
    l = jnp.sum(pr, axis=-1, keepdims=True)
    o = jnp.dot(pr.astype(_BF), vw, preferred_element_type=_F32) / l
    lse = (mx + jnp.log2(l)) * LN2
    return jnp.where(lo, o[:tq], o[tq:]), jnp.where(lo, lse[:tq], lse[tq:])


def _attn_b1_kernel(q_ref, k_ref, v_ref, c_ref, s_ref, o_ref, lse_ref, kr_ref, *, m_len, tq, win):
    half = ROPE_DIMS // 2
    for p in range(N_SLAB):
        sl = slice(p * LANES, (p + 1) * LANES)
        kr_ref[:, sl] = _rope_slab(k_ref[:, sl].astype(_F32), c_ref[...], s_ref[...], half).astype(_BF)

    row = lax.broadcasted_iota(jnp.int32, (tq, win), 0)
    col = lax.broadcasted_iota(jnp.int32, (tq, win), 1)

    def block(i, carry):
        q0 = pl.multiple_of(i * tq, tq)
        start = pl.multiple_of(jnp.clip(q0 - B_RADIUS, 0, m_len - win), B_RADIUS)
        valid = jnp.abs(row - col + (q0 - start)) <= B_RADIUS
        cq = c_ref[pl.ds(q0, tq), :]
        sq = s_ref[pl.ds(q0, tq), :]
        for p in range(N_SLAB):
            sl = slice(p * LANES, (p + 1) * LANES)
            q = _rope_slab(q_ref[pl.ds(q0, tq), sl].astype(_F32), cq, sq, half) * Q_SCALE_LOG2
            o, lse = _band_slab(q, kr_ref[pl.ds(start, win), sl], v_ref[pl.ds(start, win), sl], valid)
            o_ref[pl.ds(q0, tq), sl] = o.astype(o_ref.dtype)
            lse_ref[pl.ds(q0, tq), sl] = lse
        return carry

    lax.fori_loop(0, m_len // tq, block, 0, unroll=4)


def _attn_b1(nat, cos, sin):
    b, s, _ = nat.shape
    tq = 128
    win = min(tq + 2 * B_RADIUS, s)
    c0 = A_Q_W // B_W
    o, lse = pl.pallas_call(
        functools.partial(_attn_b1_kernel, m_len=s, tq=tq, win=win),
        grid=(b,),
        in_specs=[
            pl.BlockSpec((None, s, B_W), lambda bi: (bi, 0, c0)),
            pl.BlockSpec((None, s, B_W), lambda bi: (bi, 0, c0 + 1)),
            pl.BlockSpec((None, s, B_W), lambda bi: (bi, 0, c0 + 2)),
            _const_spec((s, LANES)),
            _const_spec((s, LANES)),
        ],
        out_specs=[pl.BlockSpec((None, s, B_W), lambda bi: (bi, 0, 0))] * 2,
        out_shape=[jax.ShapeDtypeStruct((b, s, B_W), _BF), jax.ShapeDtypeStruct((b, s, B_W), _F32)],
        scratch_shapes=[pltpu.VMEM((s, B_W), _BF)],
        compiler_params=_cparams("parallel"),
        name="attn_b1",
    )(nat, nat, nat, cos, sin)
    return o.reshape(b * s, B_W), lse.reshape(b * s, B_W)


B4_SEG = 4
B4_TQS = 32


def _b4_rope_keys(k_ref, c_ref, s_ref, kr_ref):
    half = ROPE_DIMS // 2
    for j in range(B4_SEG):
        for p in range(N_SLAB):
            sl = slice(p * LANES, (p + 1) * LANES)
            kr_ref[j, :, sl] = _rope_slab(k_ref[j, :, sl].astype(_F32), c_ref[j], s_ref[j], half).astype(_BF)


def _b4_blocks(blocks, q_ref, kr_ref, v_ref, c_ref, s_ref, o_ref, lse_ref):
    half = ROPE_DIMS // 2
    seg, tqs = B4_SEG, B4_TQS
    m16 = q_ref.shape[1]
    wks = min(tqs + 2 * (B_RADIUS // seg), m16)
    tq, win = seg * tqs, seg * wks
    row = lax.broadcasted_iota(jnp.int32, (tq, win), 0)
    col = lax.broadcasted_iota(jnp.int32, (tq, win), 1)
    for i in blocks:
        a_q = i * tqs
        a_k = min(max(a_q - B_RADIUS // seg, 0), m16 - wks)
        qpos = seg * (a_q + (row & (tqs - 1))) + (row >> (tqs.bit_length() - 1))
        kpos = seg * (a_k + (col & (wks - 1))) + (col >> (wks.bit_length() - 1))
        valid = jnp.abs(qpos - kpos) <= B_RADIUS
        cq = jnp.concatenate([c_ref[j, a_q:a_q + tqs, :] for j in range(seg)], axis=0)
        sq = jnp.concatenate([s_ref[j, a_q:a_q + tqs, :] for j in range(seg)], axis=0)
        for p in range(N_SLAB):
            sl = slice(p * LANES, (p + 1) * LANES)
            q = jnp.concatenate([q_ref[j, a_q:a_q + tqs, sl] for j in range(seg)], axis=0).astype(_F32)
            q = _rope_slab(q, cq, sq, half) * Q_SCALE_LOG2
            kw = jnp.concatenate([kr_ref[j, a_k:a_k + wks, sl] for j in range(seg)], axis=0)
            vw = jnp.concatenate([v_ref[j, a_k:a_k + wks, sl] for j in range(seg)], axis=0)
            o, lse = _band_slab(q, kw, vw, valid)
            for j in range(seg):
                o_ref[j, a_q:a_q + tqs, sl] = o[j * tqs:(j + 1) * tqs].astype(o_ref.dtype)
                lse_ref[j, a_q:a_q + tqs, sl] = lse[j * tqs:(j + 1) * tqs]


def _attn_b4_kernel(q_ref, k_ref, v_ref, c_ref, s_ref, o_ref, lse_ref, kr_ref):
    _b4_rope_keys(k_ref, c_ref, s_ref, kr_ref)
    _b4_blocks(range(q_ref.shape[1] // B4_TQS), q_ref, kr_ref, v_ref, c_ref, s_ref, o_ref, lse_ref)


def _attn_b4(perm, cos_p, sin_p):
    b, _, m16, cols = perm.shape
    pv = perm.reshape(b, B4_SEG, 4, m16, cols)
    cv = cos_p.reshape(B4_SEG, 4, m16, LANES)
    sv = sin_p.reshape(B4_SEG, 4, m16, LANES)
    blk = lambda cb: pl.BlockSpec((None, B4_SEG, None, m16, B_W), lambda bi, r: (bi, 0, r, 0, cb))
    tab = pl.BlockSpec((B4_SEG, None, m16, LANES), lambda bi, r: (0, r, 0, 0))
    o, lse = pl.pallas_call(
        _attn_b4_kernel,
        grid=(b, 4),
        in_specs=[blk(0), blk(1), blk(2), tab, tab],
        out_specs=[blk(0), blk(0)],
        out_shape=[jax.ShapeDtypeStruct((b, B4_SEG, 4, m16, B_W), _BF), jax.ShapeDtypeStruct((b, B4_SEG, 4, m16, B_W), _F32)],
        scratch_shapes=[pltpu.VMEM((B4_SEG, m16, B_W), _BF)],
        compiler_params=_cparams("parallel", "parallel"),
        name="attn_b4",
    )(pv, pv, pv, cv, sv)
    return o.reshape(b, PERM, m16, B_W), lse.reshape(b, PERM, m16, B_W)


def _b16_residues(residues, q_ref, k_ref, v_ref, c_ref, s_ref, o_ref, lse_ref):
    half = ROPE_DIMS // 2
    m16 = q_ref.shape[1]
    tq = min(128, m16)
    win = min(tq + 2 * B_RADIUS, m16)
    row = lax.broadcasted_iota(jnp.int32, (tq, win), 0)
    col = lax.broadcasted_iota(jnp.int32, (tq, win), 1)
    for r in residues:
        for i in range(m16 // tq):
            q0 = i * tq
            start = min(max(q0 - B_RADIUS, 0), m16 - win)
            valid = jnp.abs(row - col + (q0 - start)) <= B_RADIUS
            for p in range(N_SLAB):
                sl = slice(p * LANES, (p + 1) * LANES)
                q = _rope_slab(q_ref[r, q0:q0 + tq, sl].astype(_F32), c_ref[r, q0:q0 + tq, :], s_ref[r, q0:q0 + tq, :], half)
                kw = _rope_slab(k_ref[r, start:start + win, sl].astype(_F32), c_ref[r, start:start + win, :],
                                s_ref[r, start:start + win, :], half).astype(_BF)
                o, lse = _band_slab(q * Q_SCALE_LOG2, kw, v_ref[r, start:start + win, sl], valid)
                o_ref[r, q0:q0 + tq, sl] = o.astype(o_ref.dtype)
                lse_ref[r, q0:q0 + tq, sl] = lse


def _even_out_kernel(x_ref, a_ref, o0_ref, l0_ref, o1_ref, l1_ref, o2_ref, l2_ref, w_ref, y_ref):
    l1, l2 = l1_ref[...], l2_ref[...]
    m12 = jnp.maximum(l1, l2)
    e1, e2 = jnp.exp(l1 - m12), jnp.exp(l2 - m12)
    den12 = e1 + e2
    o12 = _from_residue_major((e1 * o1_ref[...].astype(_F32) + e2 * o2_ref[...].astype(_F32)) / den12)
    l12 = _from_residue_major(m12 + jnp.log(den12))
    l0 = l0_ref[...]
    mx = jnp.maximum(l0, l12)
    e0, e12 = jnp.exp(l0 - mx), jnp.exp(l12 - mx)
    bo = (e0 * o0_ref[...].astype(_F32) + e12 * o12) / (e0 + e12)
    y = jnp.dot(a_ref[...], w_ref[:A_Q_W, :], preferred_element_type=_F32)
    y = y + jnp.dot(bo.astype(_BF), w_ref[A_Q_W:, :], preferred_element_type=_F32)
    y_ref[...] = x_ref[...] + y


def _even_out(x, a_o, o0, l0, o1, l1, o2, l2, w, *, s, tm):
    n, d = x.shape
    tpb = s // tm
    row = lambda width: pl.BlockSpec((tm, width), lambda i: (i, 0))
    prm = pl.BlockSpec((None, PERM, tm // PERM, B_W), lambda i: (i // tpb, 0, i % tpb, 0))
    return pl.pallas_call(
        _even_out_kernel,
        grid=(n // tm,),
        in_specs=[row(d), row(A_Q_W), row(B_W), row(B_W), prm, prm, prm, prm, _const_spec(w.shape)],
        out_specs=row(d),
        out_shape=jax.ShapeDtypeStruct((n, d), _F32),
        compiler_params=_cparams("parallel"),
        name="even_out",
    )(x, a_o, o0, l0, o1, l1, o2, l2, w)


def _ffn_kernel(*refs, chunk, final_norm, mixer_proj):
    if mixer_proj:
        x_ref, hs_ref, gate_ref, wo_ref, g_ref, wg_ref, wu_ref, wd_ref, gf_ref, y_ref, h_ref, act_ref = refs
        rows = h_ref.shape[0]
        hs = jnp.swapaxes(hs_ref[...].astype(_F32), 0, 1).reshape(rows, D_RNN)
        a = (hs * gate_ref[...].reshape(rows, D_RNN).astype(_F32)).astype(_BF)
        x = x_ref[...].reshape(rows, D_MODEL) + jnp.dot(a, wo_ref[...], preferred_element_type=_F32)
    else:
        x_ref, g_ref, wg_ref, wu_ref, wd_ref, gf_ref, y_ref, h_ref, act_ref = refs
        x = x_ref[...]
    h_ref[...] = _rms_rows(x, g_ref[...]).astype(_BF)
    for c in range(D_FF // chunk):
        sl = slice(c * chunk, (c + 1) * chunk)
        gate = jnp.dot(h_ref[...], wg_ref[:, sl], preferred_element_type=_F32)
        up = jnp.dot(h_ref[...], wu_ref[:, sl], preferred_element_type=_F32)
        act_ref[:, sl] = (gate * _sigmoid(gate) * up).astype(_BF)
    y = x + jnp.dot(act_ref[...], wd_ref[...], preferred_element_type=_F32)
    if final_norm:
        y = _rms_rows(y, gf_ref[...])
    y_ref[...] = y.reshape(y_ref.shape)


def _ffn(x, g, w_gate_up, w_down, g_final, *, tm, final_norm, hsum_tm=None, gate=None, w_mixer=None):
    b, s, d = x.shape
    mixer_proj = hsum_tm is not None
    const = lambda shape, *idx: pl.BlockSpec(shape, lambda *_: idx, pipeline_mode=pl.Buffered(1))
    weights = [const((1, d), 0, 0), const((d, D_FF), 0, 0), const((d, D_FF), 0, 1), const((D_FF, d), 0, 0), const((1, d), 0, 0)]
    w_args = [g, w_gate_up, w_gate_up, w_down, g_final]
    if mixer_proj:
        tt = tm // SCAN_BATCH
        grid = (b // SCAN_BATCH, s // tt)
        xs = pl.BlockSpec((SCAN_BATCH, tt, d), lambda gi, c: (gi, c, 0))
        in_specs = [xs, pl.BlockSpec((None, tt, SCAN_BATCH, d), lambda gi, c: (gi, c, 0, 0)), xs, const(w_mixer.shape, 0, 0)] + weights
        args = [x, hsum_tm, gate, w_mixer] + w_args
        sem = ("parallel", "parallel")
    else:
        x = x.reshape(b * s, d)
        grid = (b * s // tm,)
        xs = pl.BlockSpec((tm, d), lambda i: (i, 0))
        in_specs = [xs] + weights
        args = [x] + w_args
        sem = ("parallel",)
    y = pl.pallas_call(
        functools.partial(_ffn_kernel, chunk=256, final_norm=final_norm, mixer_proj=mixer_proj),
        grid=grid,
        in_specs=in_specs,
        out_specs=xs,
        out_shape=jax.ShapeDtypeStruct(x.shape, _F32),
        scratch_shapes=[pltpu.VMEM((tm, d), _BF), pltpu.VMEM((tm, D_FF), _BF)],
        compiler_params=_cparams(*sem),
        name="ffn_mix" if mixer_proj else "ffn",
    )(*args)
    return y.reshape(b, s, d)


SCAN_BATCH = 8
SCAN_UNROLL = 4
HALO = 8


def _conv_kernel(u_ref, prev_ref, next_ref, w_ref, b_ref, o_ref, *, tc):
    c = pl.program_id(1)
    has_prev = (c > 0).astype(_F32)
    has_next = (c < pl.num_programs(1) - 1).astype(_F32)
    t = lax.broadcasted_iota(jnp.int32, (tc, LANES), 0)
    for cb in range(D_RNN // LANES):
        sl = slice(cb * LANES, (cb + 1) * LANES)
        w = w_ref[:, sl]
        bias = b_ref[:, sl]
        rows = []
        for bi in range(SCAN_BATCH):
            u = u_ref[bi, :, sl]
            p_last = prev_ref[bi, HALO - 1:HALO, sl] * has_prev
            n0 = next_ref[bi, 0:1, sl] * has_next
            n1 = next_ref[bi, 1:2, sl] * has_next
            um1 = jnp.where(t == 0, p_last, pltpu.roll(u, 1, 0))
            up1 = jnp.where(t == tc - 1, n0, pltpu.roll(u, tc - 1, 0))
            up2 = jnp.where(t == tc - 2, n0, jnp.where(t == tc - 1, n1, pltpu.roll(u, tc - 2, 0)))
            rows.append(um1 * w[0:1] + u * w[1:2] + up1 * w[2:3] + up2 * w[3:4] + bias)
        o_ref[:, :, sl] = jnp.swapaxes(jnp.stack(rows, axis=0), 0, 1)


def _conv(u, w, bias, *, tc):
    b, s, d = u.shape
    last = s // HALO - 1
    per = tc // HALO
    return pl.pallas_call(
        functools.partial(_conv_kernel, tc=tc),
        grid=(b // SCAN_BATCH, s // tc),
        in_specs=[
            pl.BlockSpec((SCAN_BATCH, tc, d), lambda g, c: (g, c, 0)),
            pl.BlockSpec((SCAN_BATCH, HALO, d), lambda g, c: (g, jnp.maximum(c * per - 1, 0), 0)),
            pl.BlockSpec((SCAN_BATCH, HALO, d), lambda g, c: (g, jnp.minimum((c + 1) * per, last), 0)),
            _const_spec(w.shape),
            _const_spec((1, d)),
        ],
        out_specs=pl.BlockSpec((None, tc, SCAN_BATCH, d), lambda g, c: (g, c, 0, 0)),
        out_shape=jax.ShapeDtypeStruct((b // SCAN_BATCH, s, SCAN_BATCH, d), _F32),
        compiler_params=_cparams("parallel", "parallel"),
        name="conv",
    )(u, u, u, w, bias)


def _scan_kernel(*refs, tc, reverse, finalize):
    if finalize:
        u_ref, w_ref, ba_ref, bx_ref, lam_ref, hf_ref, o_ref, a_s, b_s, h_s, carry = refs
    else:
        u_ref, w_ref, ba_ref, bx_ref, lam_ref, o_ref, a_s, b_s, h_s, carry = refs
    rows = SCAN_BATCH * tc

    @pl.when(pl.program_id(1) == 0)
    def _():
        carry[...] = jnp.zeros_like(carry)

    for n in range(LRU_BLOCKS):
        sl = slice(n * LRU_BW, (n + 1) * LRU_BW)
        lam = lam_ref[:, sl]
        c2 = (-4.0 * LOG2E) * (jnp.maximum(-lam, 0.0) + jnp.log1p(jnp.exp(-jnp.abs(lam))))
        xh = u_ref[:, :, sl].reshape(rows, LRU_BW)
        z = jnp.dot(xh.astype(_BF), w_ref[n], preferred_element_type=_F32)
        th_a = jnp.tanh(z[:, :LRU_BW] + ba_ref[:, sl])
        th_x = jnp.tanh(z[:, LRU_BW:] + bx_ref[:, sl])
        a = jnp.exp2(c2 + c2 * th_a)
        a_s[n] = a
        v = 1.0 - a * a
        root = jnp.where(v > 0.0, v * lax.rsqrt(v), 0.0)
        b_s[n] = root * ((1.0 + th_x) * xh)

    def step(i, hs):
        t = (tc - 1 - i) if reverse else i
        idx = pl.ds(pl.multiple_of(t * SCAN_BATCH, SCAN_BATCH), SCAN_BATCH)
        new = []
        for n in range(LRU_BLOCKS):
            h = a_s[n, idx, :] * hs[n] + b_s[n, idx, :]
            h_s[n, idx, :] = h
            new.append(h)
        return tuple(new)

    hs = lax.fori_loop(0, tc, step, tuple(carry[n] for n in range(LRU_BLOCKS)), unroll=SCAN_UNROLL)
    for n in range(LRU_BLOCKS):
        carry[n] = hs[n]

    for n in range(LRU_BLOCKS):
        sl = slice(n * LRU_BW, (n + 1) * LRU_BW)
        h = h_s[n].reshape(tc, SCAN_BATCH, LRU_BW)
        if finalize:
            o_ref[:, :, sl] = (hf_ref[:, :, sl] + h).astype(o_ref.dtype)
        else:
            o_ref[:, :, sl] = h


def _scan(u, w, ba, bx, lam, *, tc, reverse, h_fwd=None):
    g8, s, _, d = u.shape
    nt = s // tc
    finalize = h_fwd is not None
    tmap = (lambda g, c: (g, nt - 1 - c, 0, 0)) if reverse else (lambda g, c: (g, c, 0, 0))
    blk = pl.BlockSpec((None, tc, SCAN_BATCH, d), tmap)
    in_specs = [blk, _const_spec(w.shape), _const_spec((1, d)), _const_spec((1, d)), _const_spec((1, d))]
    args = [u, w, ba, bx, lam]
    if finalize:
        in_specs += [blk]
        args += [h_fwd]
    work = pltpu.VMEM((LRU_BLOCKS, SCAN_BATCH * tc, LRU_BW), _F32)
    return pl.pallas_call(
        functools.partial(_scan_kernel, tc=tc, reverse=reverse, finalize=finalize),
        grid=(g8, nt),
        in_specs=in_specs,
        out_specs=blk,
        out_shape=jax.ShapeDtypeStruct(u.shape, _BF if finalize else _F32),
        scratch_shapes=[work, work, work, pltpu.VMEM((LRU_BLOCKS, SCAN_BATCH, LRU_BW), _F32)],
        compiler_params=_cparams("parallel", "arbitrary"),
        name="scan_bwd" if reverse else "scan_fwd",
    )(*args)


def _rope_tables(pos_lo, pos_hi, n_rot, theta, s):
    half = n_rot // 2
    inv = jnp.power(jnp.float32(theta), -jnp.arange(half, dtype=_F32) * (2.0 / n_rot))

    def part(pos):
        ang = pos.astype(_F32)[:, None] * inv[None, :]
        c, sn = jnp.cos(ang), jnp.sin(ang)
        return jnp.concatenate([c, c], axis=-1), jnp.concatenate([-sn, sn], axis=-1)

    cs, sns = [], []
    for pos in (pos_lo, pos_hi):
        if pos is not None:
            c, sn = part(pos)
            cs.append(c)
            sns.append(sn)
    width = sum(c.shape[-1] for c in cs)
    cs.append(jnp.ones((s, HEAD_DIM - width), _F32))
    sns.append(jnp.zeros((s, HEAD_DIM - width), _F32))
    c = jnp.concatenate(cs, axis=-1)
    sn = jnp.concatenate(sns, axis=-1)
    return jnp.concatenate([c, c], axis=-1), jnp.concatenate([sn, sn], axis=-1)


def _even_w_in_layout(w):
    d = w.shape[0]
    aq = w[:, :A_Q_W].reshape(d, A_Q_HEADS, HEAD_DIM)[:, jnp.array(A_HEAD_ORDER), :].reshape(d, A_Q_W)
    akv = w[:, A_Q_W:A_Q_W + 2 * LANES]
    rest = w[:, A_Q_W + 2 * LANES:]
    w_nat = jnp.concatenate([aq, rest[:, :3 * B_W], akv], axis=1).astype(_BF)
    return w_nat, rest[:, 3 * B_W:].astype(_BF)


def _even_w_out_layout(w):
    d = w.shape[1]
    wa = w[:A_Q_W].reshape(A_Q_HEADS, HEAD_DIM, d)[jnp.array(A_HEAD_ORDER)].reshape(A_Q_W, d)
    return jnp.concatenate([wa, w[A_Q_W:]], axis=0).astype(_BF)


def kernel(x, mix_norm, ffn_norm, final_norm, even_w_in, even_q_norm, even_k_norm, even_w_out, odd_w_in, odd_conv_w, odd_conv_b, odd_gate_a_w, odd_gate_a_b, odd_gate_x_w, odd_gate_x_b, odd_lambda, odd_w_out, ffn_w_gate_up, ffn_w_down):
    b, s, d = x.shape
    n = b * s
    depth = mix_norm.shape[0]
    m16 = s // PERM
    assert d == D_MODEL and s % 512 == 0 and b % SCAN_BATCH == 0

    pos = jnp.arange(s, dtype=jnp.int32)
    cos_a, sin_a = _rope_tables(pos // GRID_W, pos % GRID_W, HEAD_DIM // 2, AXIAL_THETA, s)
    cos_b, sin_b = _rope_tables(pos, None, ROPE_DIMS, ROPE_THETA, s)
    cos_p = cos_b.reshape(m16, PERM, LANES).transpose(1, 0, 2)
    sin_p = sin_b.reshape(m16, PERM, LANES).transpose(1, 0, 2)

    tm = 512
    xf = x.reshape(n, d)
    for layer in range(depth):
        j = layer // 2
        g_mix = mix_norm[layer].reshape(1, d)
        if layer % 2 == 0:
            w_nat, w_perm = _even_w_in_layout(even_w_in[j])
            nat, perm = _even_in(xf, g_mix, w_nat, w_perm, b=b, s=s, tm=tm)
            nat3 = nat.reshape(b, s, NAT_COLS)
            gq = jnp.tile(even_q_norm[j], 2).reshape(1, LANES)
            gk = jnp.tile(even_k_norm[j], 2).reshape(1, LANES)
            a_o, o2, l2 = _attn_a(nat3, perm, cos_a, sin_a, cos_p, sin_p, gq, gk)
            o0, l0 = _attn_b1(nat3, cos_b, sin_b)
            o1, l1 = _attn_b4(perm, cos_p, sin_p)
            xf = _even_out(xf, a_o.reshape(n, A_Q_W), o0, l0, o1, l1, o2, l2, _even_w_out_layout(even_w_out[j]), s=s, tm=tm)
            mix = {}
        else:
            gate, u = _odd_in(xf, g_mix, odd_w_in[j].astype(_BF), tm=tm)
            uc = _conv(u.reshape(b, s, D_RNN), 0.5 * odd_conv_w[j], 0.5 * odd_conv_b[j].reshape(1, D_RNN), tc=128)
            h = None
            for direction in range(2):
                w = jnp.concatenate([odd_gate_a_w[j, direction], odd_gate_x_w[j, direction]], axis=-1).astype(_BF)
                h = _scan(uc, w, 0.5 * odd_gate_a_b[j, direction].reshape(1, D_RNN), 0.5 * odd_gate_x_b[j, direction].reshape(1, D_RNN),
                          odd_lambda[j, direction].reshape(1, D_RNN), tc=128, reverse=direction == 1, h_fwd=h)
            mix = dict(hsum_tm=h, gate=gate.reshape(b, s, D_RNN), w_mixer=odd_w_out[j].astype(_BF))
        xf = _ffn(xf.reshape(b, s, d), ffn_norm[layer].reshape(1, d), ffn_w_gate_up[layer].astype(_BF), ffn_w_down[layer].astype(_BF),
                  final_norm.reshape(1, d), tm=tm, final_norm=layer == depth - 1, **mix).reshape(n, d)
    return xf.reshape(b, s, d)
```

```python
import functools

import jax
import jax.numpy as jnp
from jax import lax
from jax.experimental import pallas as pl
from jax.experimental.pallas import tpu as pltpu

D_MODEL = 1024
HEAD_DIM = 64
A_Q_HEADS = 8
A_KV_HEADS = 2
B_HEADS = 8
B_RADIUS = 64
GRID_W = 64
AXIAL_THETA = 10000.0
ROPE_THETA = 500000.0
ROPE_DIMS = HEAD_DIM // 4
D_RNN = D_MODEL
LRU_BLOCKS = 8
LRU_BW = D_RNN // LRU_BLOCKS
LRU_C = 8.0
D_FF = 2816
EPS = 1e-6
NEG_INF = -1e30
LOG2E = 1.4426950408889634
LN2 = 0.6931471805599453
Q_SCALE_LOG2 = HEAD_DIM ** -0.5 * LOG2E

LANES = 128
A_Q_W = A_Q_HEADS * HEAD_DIM
B_W = B_HEADS * HEAD_DIM
N_SLAB = B_W // LANES
A_CHAIN_ROWS = 128
A_SAFE_SCORE = 50.0
A_HEAD_ORDER = (0, 4, 1, 5, 2, 6, 3, 7)
NAT_COLS = A_Q_W + 3 * B_W + 2 * LANES
A_KV_BLOCK = (A_Q_W + 3 * B_W) // LANES
PERM = 16
PROJ_CHUNK = 512
ODD_CHUNK = 256

VMEM_LIMIT = 56 * 1024 * 1024

_BF = jnp.bfloat16
_F32 = jnp.float32


def _cparams(*sem):
    return pltpu.CompilerParams(dimension_semantics=sem, vmem_limit_bytes=VMEM_LIMIT)


def _const_spec(shape):
    return pl.BlockSpec(shape, lambda *_: (0,) * len(shape), pipeline_mode=pl.Buffered(1))


def _rms_rows(x, g):
    ms = jnp.mean(x * x, axis=-1, keepdims=True)
    return x * lax.rsqrt(ms + EPS) * g


def _sigmoid(x):
    return 0.5 * (1.0 + jnp.tanh(0.5 * x))


def _lane(shape):
    return lax.broadcasted_iota(jnp.int32, shape, len(shape) - 1)


def _to_residue_major(y):
    rows, c = y.shape
    return jnp.swapaxes(y.reshape(rows // PERM, PERM, c), 0, 1).reshape(rows, c)


def _from_residue_major(y):
    p, m, c = y.shape
    return jnp.swapaxes(y, 0, 1).reshape(p * m, c)


def _even_in_kernel(x_ref, g_ref, wn_ref, wp_ref, nat_ref, perm_ref, hn_ref, hp_ref):
    cn = wn_ref.shape[1]
    hr = x_ref.shape[0] // 2
    pr = hr // PERM
    for h in range(2):
        rows = slice(h * hr, (h + 1) * hr)
        y = _rms_rows(x_ref[rows, :], g_ref[...])
        hn_ref[rows, :] = y.astype(_BF)
        hp_ref[rows, :] = _to_residue_major(y).astype(_BF)
        for c0 in range(0, cn, PROJ_CHUNK):
            sl = slice(c0, min(c0 + PROJ_CHUNK, cn))
            nat_ref[rows, sl] = jnp.dot(hn_ref[rows, :], wn_ref[:, sl], preferred_element_type=_F32).astype(_BF)
        for c in range(wp_ref.shape[1] // PROJ_CHUNK):
            sl = slice(c * PROJ_CHUNK, (c + 1) * PROJ_CHUNK)
            acc = jnp.dot(hp_ref[rows, :], wp_ref[:, sl], preferred_element_type=_F32)
            perm_ref[:, h * pr:(h + 1) * pr, sl] = acc.reshape(PERM, pr, PROJ_CHUNK).astype(_BF)


def _even_in(x, g, w_nat, w_perm, *, b, s, tm):
    n, d = x.shape
    tpb = s // tm
    cn, cp = w_nat.shape[1], w_perm.shape[1]
    return pl.pallas_call(
        _even_in_kernel,
        grid=(n // tm,),
        in_specs=[pl.BlockSpec((tm, d), lambda i: (i, 0)), _const_spec((1, d)), _const_spec((d, cn)), _const_spec((d, cp))],
        out_specs=[
            pl.BlockSpec((tm, cn), lambda i: (i, 0)),
            pl.BlockSpec((None, PERM, tm // PERM, cp), lambda i: (i // tpb, 0, i % tpb, 0)),
        ],
        out_shape=[
            jax.ShapeDtypeStruct((n, cn), _BF),
            jax.ShapeDtypeStruct((b, PERM, s // PERM, cp), _BF),
        ],
        scratch_shapes=[pltpu.VMEM((tm, d), _BF), pltpu.VMEM((tm, d), _BF)],
        compiler_params=_cparams("parallel"),
        name="even_in",
    )(x, g, w_nat, w_perm)


def _gelu_tanh(x):
    return 0.5 * x * (1.0 + jnp.tanh(0.7978845608028654 * (x + 0.044715 * (x * x * x))))


def _odd_in_kernel(x_ref, g_ref, w_ref, gate_ref, u_ref, h_ref):
    h_ref[...] = _rms_rows(x_ref[...], g_ref[...]).astype(_BF)
    for c in range(D_RNN // ODD_CHUNK):
        sl = slice(c * ODD_CHUNK, (c + 1) * ODD_CHUNK)
        u_ref[:, sl] = jnp.dot(h_ref[...], w_ref[:, D_RNN + c * ODD_CHUNK:D_RNN + (c + 1) * ODD_CHUNK],
                               preferred_element_type=_F32)
        acc = jnp.dot(h_ref[...], w_ref[:, sl], preferred_element_type=_F32)
        gate_ref[:, sl] = _gelu_tanh(acc).astype(gate_ref.dtype)


def _odd_in(x, g, w, *, tm):
    n, d = x.shape
    return pl.pallas_call(
        _odd_in_kernel,
        grid=(n // tm,),
        in_specs=[pl.BlockSpec((tm, d), lambda i: (i, 0)), _const_spec((1, d)), _const_spec(w.shape)],
        out_specs=[pl.BlockSpec((tm, D_RNN), lambda i: (i, 0)), pl.BlockSpec((tm, D_RNN), lambda i: (i, 0))],
        out_shape=[jax.ShapeDtypeStruct((n, D_RNN), _BF), jax.ShapeDtypeStruct((n, D_RNN), _F32)],
        scratch_shapes=[pltpu.VMEM((tm, d), _BF)],
        compiler_params=_cparams("parallel"),
        name="odd_in",
    )(x, g, w)


def _rope_slab(x, cos, sin_signed, half):
    lane = _lane(x.shape)
    fwd = pltpu.roll(x, LANES - half, 1)
    bwd = pltpu.roll(x, half, 1)
    partner = jnp.where((lane % (2 * half)) < half, fwd, bwd)
    return x * cos + partner * sin_signed


def _head_rms_slab(x, g):
    lo = _lane(x.shape) < HEAD_DIM
    sq = x * x
    ms_lo = jnp.sum(jnp.where(lo, sq, 0.0), axis=-1, keepdims=True) * (1.0 / HEAD_DIM)
    ms_hi = jnp.sum(jnp.where(lo, 0.0, sq), axis=-1, keepdims=True) * (1.0 / HEAD_DIM)
    rs = jnp.where(lo, lax.rsqrt(ms_lo + EPS), lax.rsqrt(ms_hi + EPS))
    return x * rs * g


def _attn_a_kernel(q_ref, k_ref, v_ref, cq_ref, sq_ref, ck_ref, sk_ref, gq_ref, gk_ref,
                   q16_ref, k16_ref, v16_ref, c16_ref, s16_ref, o_ref, o16_ref, l16_ref, kt_ref, *, tq):
    @pl.when(pl.program_id(1) == 0)
    def _():
        k = _head_rms_slab(k_ref[...].astype(_F32), gk_ref[...])
        k = _rope_slab(k, ck_ref[...], sk_ref[...], 16)
        kt_ref[...] = k.T.astype(_BF)

    lo = _lane((A_CHAIN_ROWS, LANES)) < HEAD_DIM
    gq = gq_ref[...]
    gk = gk_ref[...]
    bound_sq = (HEAD_DIM * Q_SCALE_LOG2) ** 2 * jnp.max(gq * gq) * jnp.max(gk * gk)
    bounded = bound_sq <= A_SAFE_SCORE * A_SAFE_SCORE

    n_chunks = tq // A_CHAIN_ROWS
    b16_res = q16_ref.shape[0]

    def dilated_share(ci, of):
        _b16_residues(range(ci * b16_res // of, (ci + 1) * b16_res // of),
                      q16_ref, k16_ref, v16_ref, c16_ref, s16_ref, o16_ref, l16_ref)

    def a_chunk(rows, subtract_max):
        v = v_ref[...]
        cq = cq_ref[rows, :]
        sq = sq_ref[rows, :]
        for j in range(A_Q_W // LANES):
            cols = slice(j * LANES, (j + 1) * LANES)
            q = _rope_slab(_head_rms_slab(q_ref[rows, cols].astype(_F32), gq), cq, sq, 16) * Q_SCALE_LOG2
            outs = []
            for qm in (jnp.where(lo, q, 0.0), jnp.where(lo, 0.0, q)):
                s = jnp.dot(qm.astype(_BF), kt_ref[...], preferred_element_type=_F32)
                if subtract_max:
                    s = s - jnp.max(s, axis=-1, keepdims=True)
                p = jnp.exp2(s)
                l = jnp.sum(p, axis=-1, keepdims=True)
                outs.append(jnp.dot(p.astype(_BF), v, preferred_element_type=_F32) / l)
            o_ref[rows, cols] = jnp.where(lo, outs[0], outs[1]).astype(o_ref.dtype)

    @pl.when(bounded)
    def _():
        for ci in range(n_chunks):
            a_chunk(slice(ci * A_CHAIN_ROWS, (ci + 1) * A_CHAIN_ROWS), False)
            dilated_share(ci, n_chunks)

    @pl.when(jnp.logical_not(bounded))
    def _():
        dilated_share(0, 1)

        def body(ci, carry):
            a_chunk(pl.ds(pl.multiple_of(ci * A_CHAIN_ROWS, A_CHAIN_ROWS), A_CHAIN_ROWS), True)
            return carry

        lax.fori_loop(0, n_chunks, body, 0)


A_STEPS = 4


def _attn_a(nat, perm, cos, sin, cos_p, sin_p, gq, gk):
    b, s, _ = nat.shape
    m16 = perm.shape[2]
    tq = s // A_STEPS
    res = PERM // A_STEPS
    b16 = lambda cb: pl.BlockSpec((None, res, m16, B_W), lambda bi, i: (bi, i, 0, cb))
    t16 = pl.BlockSpec((res, m16, LANES), lambda bi, i: (i, 0, 0))
    return pl.pallas_call(
        functools.partial(_attn_a_kernel, tq=tq),
        grid=(b, A_STEPS),
        in_specs=[
            pl.BlockSpec((None, tq, A_Q_W), lambda bi, i: (bi, i, 0)),
            pl.BlockSpec((None, s, LANES), lambda bi, i: (bi, 0, A_KV_BLOCK)),
            pl.BlockSpec((None, s, LANES), lambda bi, i: (bi, 0, A_KV_BLOCK + 1)),
            pl.BlockSpec((tq, LANES), lambda bi, i: (i, 0)),
            pl.BlockSpec((tq, LANES), lambda bi, i: (i, 0)),
            _const_spec((s, LANES)),
            _const_spec((s, LANES)),
            _const_spec((1, LANES)),
            _const_spec((1, LANES)),
            b16(3), b16(4), b16(5), t16, t16,
        ],
        out_specs=[pl.BlockSpec((None, tq, A_Q_W), lambda bi, i: (bi, i, 0)), b16(0), b16(0)],
        out_shape=[
            jax.ShapeDtypeStruct((b, s, A_Q_W), _BF),
            jax.ShapeDtypeStruct((b, PERM, m16, B_W), _BF),
            jax.ShapeDtypeStruct((b, PERM, m16, B_W), _F32),
        ],
        scratch_shapes=[pltpu.VMEM((LANES, s), _BF)],
        compiler_params=_cparams("parallel", "arbitrary"),
        name="attn_a",
    )(nat, nat, nat, cos, sin, cos, sin, gq, gk, perm, perm, perm, cos_p, sin_p)


def _band_slab(q, kw, vw, valid):
    tq = q.shape[0]
    lo = _lane(q.shape) < HEAD_DIM
    qm = jnp.concatenate([jnp.where(lo, q, 0.0), jnp.where(lo, 0.0, q)], axis=0).astype(_BF)
    s = lax.dot_general(qm, kw, (((1,), (1,)), ((), ())), preferred_element_type=_F32)
    s = jnp.where(jnp.concatenate([valid, valid], axis=0), s, NEG_INF)
    mx = jnp.max(s, axis=-1, keepdims=True)
    pr = jnp.exp2(s - mx)
    l = jnp.sum(pr, axis=-1, keepdims=True)
    o = jnp.dot(pr.astype(_BF), vw, preferred_element_type=_F32) / l
    lse = (mx + jnp.log2(l)) * LN2
    return jnp.where(lo, o[:tq], o[tq:]), jnp.where(lo, lse[:tq], lse[tq:])


def _attn_b1_kernel(q_ref, k_ref, v_ref, c_ref, s_ref, o_ref, lse_ref, kr_ref, *, m_len, tq, win):
    half = ROPE_DIMS // 2
    for p in range(N_SLAB):
        sl = slice(p * LANES, (p + 1) * LANES)
        kr_ref[:, sl] = _rope_slab(k_ref[:, sl].astype(_F32), c_ref[...], s_ref[...], half).astype(_BF)

    row = lax.broadcasted_iota(jnp.int32, (tq, win), 0)
    col = lax.broadcasted_iota(jnp.int32, (tq, win), 1)

    def block(i, carry):
        q0 = pl.multiple_of(i * tq, tq)
        start = pl.multiple_of(jnp.clip(q0 - B_RADIUS, 0, m_len - win), B_RADIUS)
        valid = jnp.abs(row - col + (q0 - start)) <= B_RADIUS
        cq = c_ref[pl.ds(q0, tq), :]
        sq = s_ref[pl.ds(q0, tq), :]
        for p in range(N_SLAB):
            sl = slice(p * LANES, (p + 1) * LANES)
            q = _rope_slab(q_ref[pl.ds(q0, tq), sl].astype(_F32), cq, sq, half) * Q_SCALE_LOG2
            o, lse = _band_slab(q, kr_ref[pl.ds(start, win), sl], v_ref[pl.ds(start, win), sl], valid)
            o_ref[pl.ds(q0, tq), sl] = o.astype(o_ref.dtype)
            lse_ref[pl.ds(q0, tq), sl] = lse
        return carry

    lax.fori_loop(0, m_len // tq, block, 0, unroll=4)


def _attn_b1(nat, cos, sin):
    b, s, _ = nat.shape
    tq = 128
    win = min(tq + 2 * B_RADIUS, s)
    c0 = A_Q_W // B_W
    o, lse = pl.pallas_call(
        functools.partial(_attn_b1_kernel, m_len=s, tq=tq, win=win),
        grid=(b,),
        in_specs=[
            pl.BlockSpec((None, s, B_W), lambda bi: (bi, 0, c0)),
            pl.BlockSpec((None, s, B_W), lambda bi: (bi, 0, c0 + 1)),
            pl.BlockSpec((None, s, B_W), lambda bi: (bi, 0, c0 + 2)),
            _const_spec((s, LANES)),
            _const_spec((s, LANES)),
        ],
        out_specs=[pl.BlockSpec((None, s, B_W), lambda bi: (bi, 0, 0))] * 2,
        out_shape=[jax.ShapeDtypeStruct((b, s, B_W), _BF), jax.ShapeDtypeStruct((b, s, B_W), _F32)],
        scratch_shapes=[pltpu.VMEM((s, B_W), _BF)],
        compiler_params=_cparams("parallel"),
        name="attn_b1",
    )(nat, nat, nat, cos, sin)
    return o.reshape(b * s, B_W), lse.reshape(b * s, B_W)


B4_SEG = 4
B4_TQS = 32


def _b4_rope_keys(k_ref, c_ref, s_ref, kr_ref):
    half = ROPE_DIMS // 2
    for j in range(B4_SEG):
        for p in range(N_SLAB):
            sl = slice(p * LANES, (p + 1) * LANES)
            kr_ref[j, :, sl] = _rope_slab(k_ref[j, :, sl].astype(_F32), c_ref[j], s_ref[j], half).astype(_BF)


def _b4_blocks(blocks, q_ref, kr_ref, v_ref, c_ref, s_ref, o_ref, lse_ref):
    half = ROPE_DIMS // 2
    seg, tqs = B4_SEG, B4_TQS
    m16 = q_ref.shape[1]
    wks = min(tqs + 2 * (B_RADIUS // seg), m16)
    tq, win = seg * tqs, seg * wks
    row = lax.broadcasted_iota(jnp.int32, (tq, win), 0)
    col = lax.broadcasted_iota(jnp.int32, (tq, win), 1)
    for i in blocks:
        a_q = i * tqs
        a_k = min(max(a_q - B_RADIUS // seg, 0), m16 - wks)
        qpos = seg * (a_q + (row & (tqs - 1))) + (row >> (tqs.bit_length() - 1))
        kpos = seg * (a_k + (col & (wks - 1))) + (col >> (wks.bit_length() - 1))
        valid = jnp.abs(qpos - kpos) <= B_RADIUS
        cq = jnp.concatenate([c_ref[j, a_q:a_q + tqs, :] for j in range(seg)], axis=0)
        sq = jnp.concatenate([s_ref[j, a_q:a_q + tqs, :] for j in range(seg)], axis=0)
        for p in range(N_SLAB):
            sl = slice(p * LANES, (p + 1) * LANES)
            q = jnp.concatenate([q_ref[j, a_q:a_q + tqs, sl] for j in range(seg)], axis=0).astype(_F32)
            q = _rope_slab(q, cq, sq, half) * Q_SCALE_LOG2
            kw = jnp.concatenate([kr_ref[j, a_k:a_k + wks, sl] for j in range(seg)], axis=0)
            vw = jnp.concatenate([v_ref[j, a_k:a_k + wks, sl] for j in range(seg)], axis=0)
            o, lse = _band_slab(q, kw, vw, valid)
            for j in range(seg):
                o_ref[j, a_q:a_q + tqs, sl] = o[j * tqs:(j + 1) * tqs].astype(o_ref.dtype)
                lse_ref[j, a_q:a_q + tqs, sl] = lse[j * tqs:(j + 1) * tqs]


def _attn_b4_kernel(q_ref, k_ref, v_ref, c_ref, s_ref, o_ref, lse_ref, kr_ref):
    _b4_rope_keys(k_ref, c_ref, s_ref, kr_ref)
    _b4_blocks(range(q_ref.shape[1] // B4_TQS), q_ref, kr_ref, v_ref, c_ref, s_ref, o_ref, lse_ref)


def _attn_b4(perm, cos_p, sin_p):
    b, _, m16, cols = perm.shape
    pv = perm.reshape(b, B4_SEG, 4, m16, cols)
    cv = cos_p.reshape(B4_SEG, 4, m16, LANES)
    sv = sin_p.reshape(B4_SEG, 4, m16, LANES)
    blk = lambda cb: pl.BlockSpec((None, B4_SEG, None, m16, B_W), lambda bi, r: (bi, 0, r, 0, cb))
    tab = pl.BlockSpec((B4_SEG, None, m16, LANES), lambda bi, r: (0, r, 0, 0))
    o, lse = pl.pallas_call(
        _attn_b4_kernel,
        grid=(b, 4),
        in_specs=[blk(0), blk(1), blk(2), tab, tab],
        out_specs=[blk(0), blk(0)],
        out_shape=[jax.ShapeDtypeStruct((b, B4_SEG, 4, m16, B_W), _BF), jax.ShapeDtypeStruct((b, B4_SEG, 4, m16, B_W), _F32)],
        scratch_shapes=[pltpu.VMEM((B4_SEG, m16, B_W), _BF)],
        compiler_params=_cparams("parallel", "parallel"),
        name="attn_b4",
    )(pv, pv, pv, cv, sv)
    return o.reshape(b, PERM, m16, B_W), lse.reshape(b, PERM, m16, B_W)


def _b16_residues(residues, q_ref, k_ref, v_ref, c_ref, s_ref, o_ref, lse_ref):
    half = ROPE_DIMS // 2
    m16 = q_ref.shape[1]
    tq = min(128, m16)
    win = min(tq + 2 * B_RADIUS, m16)
    row = lax.broadcasted_iota(jnp.int32, (tq, win), 0)
    col = lax.broadcasted_iota(jnp.int32, (tq, win), 1)
    for r in residues:
        for i in range(m16 // tq):
            q0 = i * tq
            start = min(max(q0 - B_RADIUS, 0), m16 - win)
            valid = jnp.abs(row - col + (q0 - start)) <= B_RADIUS
            for p in range(N_SLAB):
                sl = slice(p * LANES, (p + 1) * LANES)
                q = _rope_slab(q_ref[r, q0:q0 + tq, sl].astype(_F32), c_ref[r, q0:q0 + tq, :], s_ref[r, q0:q0 + tq, :], half)
                kw = _rope_slab(k_ref[r, start:start + win, sl].astype(_F32), c_ref[r, start:start + win, :],
                                s_ref[r, start:start + win, :], half).astype(_BF)
                o, lse = _band_slab(q * Q_SCALE_LOG2, kw, v_ref[r, start:start + win, sl], valid)
                o_ref[r, q0:q0 + tq, sl] = o.astype(o_ref.dtype)
                lse_ref[r, q0:q0 + tq, sl] = lse


def _even_out_kernel(x_ref, a_ref, o0_ref, l0_ref, o1_ref, l1_ref, o2_ref, l2_ref, w_ref, y_ref):
    l1, l2 = l1_ref[...], l2_ref[...]
    m12 = jnp.maximum(l1, l2)
    e1, e2 = jnp.exp(l1 - m12), jnp.exp(l2 - m12)
    den12 = e1 + e2
    o12 = _from_residue_major((e1 * o1_ref[...].astype(_F32) + e2 * o2_ref[...].astype(_F32)) / den12)
    l12 = _from_residue_major(m12 + jnp.log(den12))
    l0 = l0_ref[...]
    mx = jnp.maximum(l0, l12)
    e0, e12 = jnp.exp(l0 - mx), jnp.exp(l12 - mx)
    bo = (e0 * o0_ref[...].astype(_F32) + e12 * o12) / (e0 + e12)
    y = jnp.dot(a_ref[...], w_ref[:A_Q_W, :], preferred_element_type=_F32)
    y = y + jnp.dot(bo.astype(_BF), w_ref[A_Q_W:, :], preferred_element_type=_F32)
    y_ref[...] = x_ref[...] + y


def _even_out(x, a_o, o0, l0, o1, l1, o2, l2, w, *, s, tm):
    n, d = x.shape
    tpb = s // tm
    row = lambda width: pl.BlockSpec((tm, width), lambda i: (i, 0))
    prm = pl.BlockSpec((None, PERM, tm // PERM, B_W), lambda i: (i // tpb, 0, i % tpb, 0))
    return pl.pallas_call(
        _even_out_kernel,
        grid=(n // tm,),
        in_specs=[row(d), row(A_Q_W), row(B_W), row(B_W), prm, prm, prm, prm, _const_spec(w.shape)],
        out_specs=row(d),
        out_shape=jax.ShapeDtypeStruct((n, d), _F32),
        compiler_params=_cparams("parallel"),
        name="even_out",
    )(x, a_o, o0, l0, o1, l1, o2, l2, w)


def _ffn_kernel(*refs, chunk, final_norm, mixer_proj):
    if mixer_proj:
        x_ref, hs_ref, gate_ref, wo_ref, g_ref, wg_ref, wu_ref, wd_ref, gf_ref, y_ref, h_ref, act_ref = refs
        rows = h_ref.shape[0]
        hs = jnp.swapaxes(hs_ref[...].astype(_F32), 0, 1).reshape(rows, D_RNN)
        a = (hs * gate_ref[...].reshape(rows, D_RNN).astype(_F32)).astype(_BF)
        x = x_ref[...].reshape(rows, D_MODEL) + jnp.dot(a, wo_ref[...], preferred_element_type=_F32)
    else:
        x_ref, g_ref, wg_ref, wu_ref, wd_ref, gf_ref, y_ref, h_ref, act_ref = refs
        x = x_ref[...]
    h_ref[...] = _rms_rows(x, g_ref[...]).astype(_BF)
    for c in range(D_FF // chunk):
        sl = slice(c * chunk, (c + 1) * chunk)
        gate = jnp.dot(h_ref[...], wg_ref[:, sl], preferred_element_type=_F32)
        up = jnp.dot(h_ref[...], wu_ref[:, sl], preferred_element_type=_F32)
        act_ref[:, sl] = (gate * _sigmoid(gate) * up).astype(_BF)
    y = x + jnp.dot(act_ref[...], wd_ref[...], preferred_element_type=_F32)
    if final_norm:
        y = _rms_rows(y, gf_ref[...])
    y_ref[...] = y.reshape(y_ref.shape)


def _ffn(x, g, w_gate_up, w_down, g_final, *, tm, final_norm, hsum_tm=None, gate=None, w_mixer=None):
    b, s, d = x.shape
    mixer_proj = hsum_tm is not None
    const = lambda shape, *idx: pl.BlockSpec(shape, lambda *_: idx, pipeline_mode=pl.Buffered(1))
    weights = [const((1, d), 0, 0), const((d, D_FF), 0, 0), const((d, D_FF), 0, 1), const((D_FF, d), 0, 0), const((1, d), 0, 0)]
    w_args = [g, w_gate_up, w_gate_up, w_down, g_final]
    if mixer_proj:
        tt = tm // SCAN_BATCH
        grid = (b // SCAN_BATCH, s // tt)
        xs = pl.BlockSpec((SCAN_BATCH, tt, d), lambda gi, c: (gi, c, 0))
        in_specs = [xs, pl.BlockSpec((None, tt, SCAN_BATCH, d), lambda gi, c: (gi, c, 0, 0)), xs, const(w_mixer.shape, 0, 0)] + weights
        args = [x, hsum_tm, gate, w_mixer] + w_args
        sem = ("parallel", "parallel")
    else:
        x = x.reshape(b * s, d)
        grid = (b * s // tm,)
        xs = pl.BlockSpec((tm, d), lambda i: (i, 0))
        in_specs = [xs] + weights
        args = [x] + w_args
        sem = ("parallel",)
    y = pl.pallas_call(
        functools.partial(_ffn_kernel, chunk=256, final_norm=final_norm, mixer_proj=mixer_proj),
        grid=grid,
        in_specs=in_specs,
        out_specs=xs,
        out_shape=jax.ShapeDtypeStruct(x.shape, _F32),
        scratch_shapes=[pltpu.VMEM((tm, d), _BF), pltpu.VMEM((tm, D_FF), _BF)],
        compiler_params=_cparams(*sem),
        name="ffn_mix" if mixer_proj else "ffn",
    )(*args)
    return y.reshape(b, s, d)


SCAN_BATCH = 8
SCAN_UNROLL = 4
HALO = 8


def _conv_kernel(u_ref, prev_ref, next_ref, w_ref, b_ref, o_ref, *, tc):
    c = pl.program_id(1)
    has_prev = (c > 0).astype(_F32)
    has_next = (c < pl.num_programs(1) - 1).astype(_F32)
    t = lax.broadcasted_iota(jnp.int32, (tc, LANES), 0)
    for cb in range(D_RNN // LANES):
        sl = slice(cb * LANES, (cb + 1) * LANES)
        w = w_ref[:, sl]
        bias = b_ref[:, sl]
        rows = []
        for bi in range(SCAN_BATCH):
            u = u_ref[bi, :, sl]
            p_last = prev_ref[bi, HALO - 1:HALO, sl] * has_prev
            n0 = next_ref[bi, 0:1, sl] * has_next
            n1 = next_ref[bi, 1:2, sl] * has_next
            um1 = jnp.where(t == 0, p_last, pltpu.roll(u, 1, 0))
            up1 = jnp.where(t == tc - 1, n0, pltpu.roll(u, tc - 1, 0))
            up2 = jnp.where(t == tc - 2, n0, jnp.where(t == tc - 1, n1, pltpu.roll(u, tc - 2, 0)))
            rows.append(um1 * w[0:1] + u * w[1:2] + up1 * w[2:3] + up2 * w[3:4] + bias)
        o_ref[:, :, sl] = jnp.swapaxes(jnp.stack(rows, axis=0), 0, 1)


def _conv(u, w, bias, *, tc):
    b, s, d = u.shape
    last = s // HALO - 1
    per = tc // HALO
    return pl.pallas_call(
        functools.partial(_conv_kernel, tc=tc),
        grid=(b // SCAN_BATCH, s // tc),
        in_specs=[
            pl.BlockSpec((SCAN_BATCH, tc, d), lambda g, c: (g, c, 0)),
            pl.BlockSpec((SCAN_BATCH, HALO, d), lambda g, c: (g, jnp.maximum(c * per - 1, 0), 0)),
            pl.BlockSpec((SCAN_BATCH, HALO, d), lambda g, c: (g, jnp.minimum((c + 1) * per, last), 0)),
            _const_spec(w.shape),
            _const_spec((1, d)),
        ],
        out_specs=pl.BlockSpec((None, tc, SCAN_BATCH, d), lambda g, c: (g, c, 0, 0)),
        out_shape=jax.ShapeDtypeStruct((b // SCAN_BATCH, s, SCAN_BATCH, d), _F32),
        compiler_params=_cparams("parallel", "parallel"),
        name="conv",
    )(u, u, u, w, bias)


def _scan_kernel(*refs, tc, reverse, finalize):
    if finalize:
        u_ref, w_ref, ba_ref, bx_ref, lam_ref, hf_ref, o_ref, a_s, b_s, h_s, carry = refs
    else:
        u_ref, w_ref, ba_ref, bx_ref, lam_ref, o_ref, a_s, b_s, h_s, carry = refs
    rows = SCAN_BATCH * tc

    @pl.when(pl.program_id(1) == 0)
    def _():
        carry[...] = jnp.zeros_like(carry)

    for n in range(LRU_BLOCKS):
        sl = slice(n * LRU_BW, (n + 1) * LRU_BW)
        lam = lam_ref[:, sl]
        c2 = (-4.0 * LOG2E) * (jnp.maximum(-lam, 0.0) + jnp.log1p(jnp.exp(-jnp.abs(lam))))
        xh = u_ref[:, :, sl].reshape(rows, LRU_BW)
        z = jnp.dot(xh.astype(_BF), w_ref[n], preferred_element_type=_F32)
        th_a = jnp.tanh(z[:, :LRU_BW] + ba_ref[:, sl])
        th_x = jnp.tanh(z[:, LRU_BW:] + bx_ref[:, sl])
        a = jnp.exp2(c2 + c2 * th_a)
        a_s[n] = a
        v = 1.0 - a * a
        root = jnp.where(v > 0.0, v * lax.rsqrt(v), 0.0)
        b_s[n] = root * ((1.0 + th_x) * xh)

    def step(i, hs):
        t = (tc - 1 - i) if reverse else i
        idx = pl.ds(pl.multiple_of(t * SCAN_BATCH, SCAN_BATCH), SCAN_BATCH)
        new = []
        for n in range(LRU_BLOCKS):
            h = a_s[n, idx, :] * hs[n] + b_s[n, idx, :]
            h_s[n, idx, :] = h
            new.append(h)
        return tuple(new)

    hs = lax.fori_loop(0, tc, step, tuple(carry[n] for n in range(LRU_BLOCKS)), unroll=SCAN_UNROLL)
    for n in range(LRU_BLOCKS):
        carry[n] = hs[n]

    for n in range(LRU_BLOCKS):
        sl = slice(n * LRU_BW, (n + 1) * LRU_BW)
        h = h_s[n].reshape(tc, SCAN_BATCH, LRU_BW)
        if finalize:
            o_ref[:, :, sl] = (hf_ref[:, :, sl] + h).astype(o_ref.dtype)
        else:
            o_ref[:, :, sl] = h


def _scan(u, w, ba, bx, lam, *, tc, reverse, h_fwd=None):
    g8, s, _, d = u.shape
    nt = s // tc
    finalize = h_fwd is not None
    tmap = (lambda g, c: (g, nt - 1 - c, 0, 0)) if reverse else (lambda g, c: (g, c, 0, 0))
    blk = pl.BlockSpec((None, tc, SCAN_BATCH, d), tmap)
    in_specs = [blk, _const_spec(w.shape), _const_spec((1, d)), _const_spec((1, d)), _const_spec((1, d))]
    args = [u, w, ba, bx, lam]
    if finalize:
        in_specs += [blk]
        args += [h_fwd]
    work = pltpu.VMEM((LRU_BLOCKS, SCAN_BATCH * tc, LRU_BW), _F32)
    return pl.pallas_call(
        functools.partial(_scan_kernel, tc=tc, reverse=reverse, finalize=finalize),
        grid=(g8, nt),
        in_specs=in_specs,
        out_specs=blk,
        out_shape=jax.ShapeDtypeStruct(u.shape, _BF if finalize else _F32),
        scratch_shapes=[work, work, work, pltpu.VMEM((LRU_BLOCKS, SCAN_BATCH, LRU_BW), _F32)],
        compiler_params=_cparams("parallel", "arbitrary"),
        name="scan_bwd" if reverse else "scan_fwd",
    )(*args)


def _rope_tables(pos_lo, pos_hi, n_rot, theta, s):
    half = n_rot // 2
    inv = jnp.power(jnp.float32(theta), -jnp.arange(half, dtype=_F32) * (2.0 / n_rot))

    def part(pos):
        ang = pos.astype(_F32)[:, None] * inv[None, :]
        c, sn = jnp.cos(ang), jnp.sin(ang)
        return jnp.concatenate([c, c], axis=-1), jnp.concatenate([-sn, sn], axis=-1)

    cs, sns = [], []
    for pos in (pos_lo, pos_hi):
        if pos is not None:
            c, sn = part(pos)
            cs.append(c)
            sns.append(sn)
    width = sum(c.shape[-1] for c in cs)
    cs.append(jnp.ones((s, HEAD_DIM - width), _F32))
    sns.append(jnp.zeros((s, HEAD_DIM - width), _F32))
    c = jnp.concatenate(cs, axis=-1)
    sn = jnp.concatenate(sns, axis=-1)
    return jnp.concatenate([c, c], axis=-1), jnp.concatenate([sn, sn], axis=-1)


def _even_w_in_layout(w):
    d = w.shape[0]
    aq = w[:, :A_Q_W].reshape(d, A_Q_HEADS, HEAD_DIM)[:, jnp.array(A_HEAD_ORDER), :].reshape(d, A_Q_W)
    akv = w[:, A_Q_W:A_Q_W + 2 * LANES]
    rest = w[:, A_Q_W + 2 * LANES:]
    w_nat = jnp.concatenate([aq, rest[:, :3 * B_W], akv], axis=1).astype(_BF)
    return w_nat, rest[:, 3 * B_W:].astype(_BF)


def _even_w_out_layout(w):
    d = w.shape[1]
    wa = w[:A_Q_W].reshape(A_Q_HEADS, HEAD_DIM, d)[jnp.array(A_HEAD_ORDER)].reshape(A_Q_W, d)
    return jnp.concatenate([wa, w[A_Q_W:]], axis=0).astype(_BF)


def kernel(x, mix_norm, ffn_norm, final_norm, even_w_in, even_q_norm, even_k_norm, even_w_out, odd_w_in, odd_conv_w, odd_conv_b, odd_gate_a_w, odd_gate_a_b, odd_gate_x_w, odd_gate_x_b, odd_lambda, odd_w_out, ffn_w_gate_up, ffn_w_down):
    b, s, d = x.shape
    n = b * s
    depth = mix_norm.shape[0]
    m16 = s // PERM
    assert d == D_MODEL and s % 512 == 0 and b % SCAN_BATCH == 0

    pos = jnp.arange(s, dtype=jnp.int32)
    cos_a, sin_a = _rope_tables(pos // GRID_W, pos % GRID_W, HEAD_DIM // 2, AXIAL_THETA, s)
    cos_b, sin_b = _rope_tables(pos, None, ROPE_DIMS, ROPE_THETA, s)
    cos_p = cos_b.reshape(m16, PERM, LANES).transpose(1, 0, 2)
    sin_p = sin_b.reshape(m16, PERM, LANES).transpose(1, 0, 2)

    tm = 512
    xf = x.reshape(n, d)
    for layer in range(depth):
        j = layer // 2
        g_mix = mix_norm[layer].reshape(1, d)
        if layer % 2 == 0:
            w_nat, w_perm = _even_w_in_layout(even_w_in[j])
            nat, perm = _even_in(xf, g_mix, w_nat, w_perm, b=b, s=s, tm=tm)
            nat3 = nat.reshape(b, s, NAT_COLS)
            gq = jnp.tile(even_q_norm[j], 2).reshape(1, LANES)
            gk = jnp.tile(even_k_norm[j], 2).reshape(1, LANES)
            a_o, o2, l2 = _attn_a(nat3, perm, cos_a, sin_a, cos_p, sin_p, gq, gk)
            o0, l0 = _attn_b1(nat3, cos_b, sin_b)
            o1, l1 = _attn_b4(perm, cos_p, sin_p)
            xf = _even_out(xf, a_o.reshape(n, A_Q_W), o0, l0, o1, l1, o2, l2, _even_w_out_layout(even_w_out[j]), s=s, tm=tm)
            mix = {}
        else:
            gate, u = _odd_in(xf, g_mix, odd_w_in[j].astype(_BF), tm=tm)
            uc = _conv(u.reshape(b, s, D_RNN), 0.5 * odd_conv_w[j], 0.5 * odd_conv_b[j].reshape(1, D_RNN), tc=128)
            h = None
            for direction in range(2):
                w = jnp.concatenate([odd_gate_a_w[j, direction], odd_gate_x_w[j, direction]], axis=-1).astype(_BF)
                h = _scan(uc, w, 0.5 * odd_gate_a_b[j, direction].reshape(1, D_RNN), 0.5 * odd_gate_x_b[j, direction].reshape(1, D_RNN),
                          odd_lambda[j, direction].reshape(1, D_RNN), tc=128, reverse=direction == 1, h_fwd=h)
            mix = dict(hsum_tm=h, gate=gate.reshape(b, s, D_RNN), w_mixer=odd_w_out[j].astype(_BF))
        xf = _ffn(xf.reshape(b, s, d), ffn_norm[layer].reshape(1, d), ffn_w_gate_up[layer].astype(_BF), ffn_w_down[layer].astype(_BF),
                  final_norm.reshape(1, d), tm=tm, final_norm=layer == depth - 1, **mix).reshape(n, d)
    return xf.reshape(b, s, d)
```

```python
import functools

import jax
import jax.numpy as jnp
from jax import lax
from jax.experimental import pallas as pl
from jax.experimental.pallas import tpu as pltpu

D_MODEL = 1024
HEAD_DIM = 64
A_Q_HEADS = 8
A_KV_HEADS = 2
B_HEADS = 8
B_RADIUS = 64
GRID_W = 64
AXIAL_THETA = 10000.0
ROPE_THETA = 500000.0
ROPE_DIMS = HEAD_DIM // 4
AXIAL_HALF = HEAD_DIM // 4
D_RNN = D_MODEL
LRU_BLOCKS = 8
LRU_BW = D_RNN // LRU_BLOCKS
LRU_C = 8.0
D_FF = 2816
EPS = 1e-6
NEG_INF = -1e30
LOG2E = 1.4426950408889634
LN2 = 0.6931471805599453
Q_SCALE_LOG2 = HEAD_DIM ** -0.5 * LOG2E

LANES = 128
A_Q_W = A_Q_HEADS * HEAD_DIM
B_W = B_HEADS * HEAD_DIM
N_SLAB = B_W // LANES
A_CHAIN_ROWS = 128
A_SAFE_SCORE = 50.0
A_HEAD_ORDER = (0, 4, 1, 5, 2, 6, 3, 7)
NAT_COLS = A_Q_W + 3 * B_W + 2 * LANES
A_KV_BLOCK = (A_Q_W + 3 * B_W) // LANES
PERM = 16
PROJ_CHUNK = 512
ODD_CHUNK = 256

VMEM_LIMIT = 56 * 1024 * 1024

_BF = jnp.bfloat16
_F32 = jnp.float32


def _cparams(*sem):
    return pltpu.CompilerParams(dimension_semantics=sem, vmem_limit_bytes=VMEM_LIMIT)


def _const_spec(shape):
    return pl.BlockSpec(shape, lambda *_: (0,) * len(shape), pipeline_mode=pl.Buffered(1))


def _rms_rows(x, g):
    ms = jnp.mean(x * x, axis=-1, keepdims=True)
    return x * lax.rsqrt(ms + EPS) * g


def _sigmoid(x):
    return 0.5 * (1.0 + jnp.tanh(0.5 * x))


def _lane(shape):
    return lax.broadcasted_iota(jnp.int32, shape, len(shape) - 1)


def _to_residue_major(y):
    rows, c = y.shape
    return jnp.swapaxes(y.reshape(rows // PERM, PERM, c), 0, 1).reshape(rows, c)


def _from_residue_major(y):
    p, m, c = y.shape
    return jnp.swapaxes(y, 0, 1).reshape(p * m, c)


def _even_in_kernel(x_ref, g_ref, wn_ref, wp_ref, nat_ref, perm_ref, hn_ref, hp_ref):
    cn = wn_ref.shape[1]
    hr = x_ref.shape[0] // 2
    pr = hr // PERM
    for h in range(2):
        rows = slice(h * hr, (h + 1) * hr)
        y = _rms_rows(x_ref[rows, :], g_ref[...])
        hn_ref[rows, :] = y.astype(_BF)
        hp_ref[rows, :] = _to_residue_major(y).astype(_BF)
        for c0 in range(0, cn, PROJ_CHUNK):
            sl = slice(c0, min(c0 + PROJ_CHUNK, cn))
            nat_ref[rows, sl] = jnp.dot(hn_ref[rows, :], wn_ref[:, sl], preferred_element_type=_F32).astype(_BF)
        for c in range(wp_ref.shape[1] // PROJ_CHUNK):
            sl = slice(c * PROJ_CHUNK, (c + 1) * PROJ_CHUNK)
            acc = jnp.dot(hp_ref[rows, :], wp_ref[:, sl], preferred_element_type=_F32)
            perm_ref[:, h * pr:(h + 1) * pr, sl] = acc.reshape(PERM, pr, PROJ_CHUNK).astype(_BF)


def _even_in(x, g, w_nat, w_perm, *, b, s, tm):
    n, d = x.shape
    tpb = s // tm
    cn, cp = w_nat.shape[1], w_perm.shape[1]
    return pl.pallas_call(
        _even_in_kernel,
        grid=(n // tm,),
        in_specs=[pl.BlockSpec((tm, d), lambda i: (i, 0)), _const_spec((1, d)), _const_spec((d, cn)), _const_spec((d, cp))],
        out_specs=[
            pl.BlockSpec((tm, cn), lambda i: (i, 0)),
            pl.BlockSpec((None, PERM, tm // PERM, cp), lambda i: (i // tpb, 0, i % tpb, 0)),
        ],
        out_shape=[
            jax.ShapeDtypeStruct((n, cn), _BF),
            jax.ShapeDtypeStruct((b, PERM, s // PERM, cp), _BF),
        ],
        scratch_shapes=[pltpu.VMEM((tm, d), _BF), pltpu.VMEM((tm, d), _BF)],
        compiler_params=_cparams("parallel"),
        name="even_in",
    )(x, g, w_nat, w_perm)


def _gelu_tanh(x):
    return 0.5 * x * (1.0 + jnp.tanh(0.7978845608028654 * (x + 0.044715 * (x * x * x))))


def _odd_in_kernel(x_ref, g_ref, w_ref, gate_ref, u_ref, h_ref):
    h_ref[...] = _rms_rows(x_ref[...], g_ref[...]).astype(_BF)
    for c in range(D_RNN // ODD_CHUNK):
        sl = slice(c * ODD_CHUNK, (c + 1) * ODD_CHUNK)
        u_ref[:, sl] = jnp.dot(h_ref[...], w_ref[:, D_RNN + c * ODD_CHUNK:D_RNN + (c + 1) * ODD_CHUNK],
                               preferred_element_type=_F32)
        acc = jnp.dot(h_ref[...], w_ref[:, sl], preferred_element_type=_F32)
        gate_ref[:, sl] = _gelu_tanh(acc).astype(gate_ref.dtype)


def _odd_in(x, g, w, *, tm):
    n, d = x.shape
    return pl.pallas_call(
        _odd_in_kernel,
        grid=(n // tm,),
        in_specs=[pl.BlockSpec((tm, d), lambda i: (i, 0)), _const_spec((1, d)), _const_spec(w.shape)],
        out_specs=[pl.BlockSpec((tm, D_RNN), lambda i: (i, 0)), pl.BlockSpec((tm, D_RNN), lambda i: (i, 0))],
        out_shape=[jax.ShapeDtypeStruct((n, D_RNN), _BF), jax.ShapeDtypeStruct((n, D_RNN), _F32)],
        scratch_shapes=[pltpu.VMEM((tm, d), _BF)],
        compiler_params=_cparams("parallel"),
        name="odd_in",
    )(x, g, w)


def _rope_slab(x, cos, sin_signed, half):
    lane = _lane(x.shape)
    fwd = pltpu.roll(x, LANES - half, 1)
    bwd = pltpu.roll(x, half, 1)
    partner = jnp.where((lane % (2 * half)) < half, fwd, bwd)
    return x * cos + partner * sin_signed


def _head_rms_slab(x, g):
    lo = _lane(x.shape) < HEAD_DIM
    sq = x * x
    ms_lo = jnp.sum(jnp.where(lo, sq, 0.0), axis=-1, keepdims=True) * (1.0 / HEAD_DIM)
    ms_hi = jnp.sum(jnp.where(lo, 0.0, sq), axis=-1, keepdims=True) * (1.0 / HEAD_DIM)
    rs = jnp.where(lo, lax.rsqrt(ms_lo + EPS), lax.rsqrt(ms_hi + EPS))
    return x * rs * g


def _attn_a_kernel(q_ref, k_ref, v_ref, cq_ref, sq_ref, ck_ref, sk_ref, gq_ref, gk_ref,
                   q16_ref, k16_ref, v16_ref, c16_ref, s16_ref, o_ref, o16_ref, l16_ref, kt_ref, *, tq):
    @pl.when(pl.program_id(1) == 0)
    def _():
        k = _head_rms_slab(k_ref[...].astype(_F32), gk_ref[...])
        k = _rope_slab(k, ck_ref[...], sk_ref[...], AXIAL_HALF)
        kt_ref[...] = k.T.astype(_BF)

    lo = _lane((A_CHAIN_ROWS, LANES)) < HEAD_DIM
    gq = gq_ref[...]
    gk = gk_ref[...]
    bound_sq = (HEAD_DIM * Q_SCALE_LOG2) ** 2 * jnp.max(gq * gq) * jnp.max(gk * gk)
    bounded = bound_sq <= A_SAFE_SCORE * A_SAFE_SCORE

    n_chunks = tq // A_CHAIN_ROWS
    b16_res = q16_ref.shape[0]

    def dilated_share(ci, of):
        _b16_residues(range(ci * b16_res // of, (ci + 1) * b16_res // of),
                      q16_ref, k16_ref, v16_ref, c16_ref, s16_ref, o16_ref, l16_ref)

    def a_chunk(rows, subtract_max):
        v = v_ref[...]
        cq = cq_ref[rows, :]
        sq = sq_ref[rows, :]
        for j in range(A_Q_W // LANES):
            cols = slice(j * LANES, (j + 1) * LANES)
            q = _rope_slab(_head_rms_slab(q_ref[rows, cols].astype(_F32), gq), cq, sq, AXIAL_HALF) * Q_SCALE_LOG2
            outs = []
            for qm in (jnp.where(lo, q, 0.0), jnp.where(lo, 0.0, q)):
                s = jnp.dot(qm.astype(_BF), kt_ref[...], preferred_element_type=_F32)
                if subtract_max:
                    s = s - jnp.max(s, axis=-1, keepdims=True)
                p = jnp.exp2(s)
                l = jnp.sum(p, axis=-1, keepdims=True)
                outs.append(jnp.dot(p.astype(_BF), v, preferred_element_type=_F32) / l)
            o_ref[rows, cols] = jnp.where(lo, outs[0], outs[1]).astype(o_ref.dtype)

    @pl.when(bounded)
    def _():
        for ci in range(n_chunks):
            a_chunk(slice(ci * A_CHAIN_ROWS, (ci + 1) * A_CHAIN_ROWS), False)
            dilated_share(ci, n_chunks)

    @pl.when(jnp.logical_not(bounded))
    def _():
        dilated_share(0, 1)

        def body(ci, carry):
            a_chunk(pl.ds(pl.multiple_of(ci * A_CHAIN_ROWS, A_CHAIN_ROWS), A_CHAIN_ROWS), True)
            return carry

        lax.fori_loop(0, n_chunks, body, 0)


A_STEPS = 4


def _attn_a(nat, perm, cos, sin, cos_p, sin_p, gq, gk):
    b, s, _ = nat.shape
    m16 = perm.shape[2]
    tq = s // A_STEPS
    res = PERM // A_STEPS
    b16 = lambda cb: pl.BlockSpec((None, res, m16, B_W), lambda bi, i: (bi, i, 0, cb))
    t16 = pl.BlockSpec((res, m16, LANES), lambda bi, i: (i, 0, 0))
    return pl.pallas_call(
        functools.partial(_attn_a_kernel, tq=tq),
        grid=(b, A_STEPS),
        in_specs=[
            pl.BlockSpec((None, tq, A_Q_W), lambda bi, i: (bi, i, 0)),
            pl.BlockSpec((None, s, LANES), lambda bi, i: (bi, 0, A_KV_BLOCK)),
            pl.BlockSpec((None, s, LANES), lambda bi, i: (bi, 0, A_KV_BLOCK + 1)),
            pl.BlockSpec((tq, LANES), lambda bi, i: (i, 0)),
            pl.BlockSpec((tq, LANES), lambda bi, i: (i, 0)),
            _const_spec((s, LANES)),
            _const_spec((s, LANES)),
            _const_spec((1, LANES)),
            _const_spec((1, LANES)),
            b16(3), b16(4), b16(5), t16, t16,
        ],
        out_specs=[pl.BlockSpec((None, tq, A_Q_W), lambda bi, i: (bi, i, 0)), b16(0), b16(0)],
        out_shape=[
            jax.ShapeDtypeStruct((b, s, A_Q_W), _BF),
            jax.ShapeDtypeStruct((b, PERM, m16, B_W), _BF),
            jax.ShapeDtypeStruct((b, PERM, m16, B_W), _F32),
        ],
        scratch_shapes=[pltpu.VMEM((LANES, s), _BF)],
        compiler_params=_cparams("parallel", "arbitrary"),
        name="attn_a",
    )(nat, nat, nat, cos, sin, cos, sin, gq, gk, perm, perm, perm, cos_p, sin_p)


def _band_slab(q, kw, vw, valid):
    tq = q.shape[0]
    lo = _lane(q.shape) < HEAD_DIM
    zero = jnp.zeros((), _BF)
    qm = jnp.concatenate([jnp.where(lo, q, zero), jnp.where(lo, zero, q)], axis=0)
    s = lax.dot_general(qm, kw, (((1,), (1,)), ((), ())), preferred_element_type=_F32)
    s = jnp.where(jnp.concatenate([valid, valid], axis=0), s, NEG_INF)
    mx = jnp.max(s, axis=-1, keepdims=True)
    pr = jnp.exp2(s - mx)
    l = jnp.sum(pr, axis=-1, keepdims=True)
    o = jnp.dot(pr.astype(_BF), vw, preferred_element_type=_F32) / l
    lse = (mx + jnp.log2(l)) * LN2
    return jnp.where(lo, o[:tq], o[tq:]), jnp.where(lo, lse[:tq], lse[tq:])


def _attn_b1_kernel(q_ref, k_ref, v_ref, c_ref, s_ref, o_ref, lse_ref, kr_ref, qr_ref, *, m_len, tq, win):
    half = ROPE_DIMS // 2
    for p in range(N_SLAB):
        sl = slice(p * LANES, (p + 1) * LANES)
        kr_ref[:, sl] = _rope_slab(k_ref[:, sl].astype(_F32), c_ref[...], s_ref[...], half).astype(_BF)
        qr_ref[:, sl] = (_rope_slab(q_ref[:, sl].astype(_F32), c_ref[...], s_ref[...], half) * Q_SCALE_LOG2).astype(_BF)

    row = lax.broadcasted_iota(jnp.int32, (tq, win), 0)
    col = lax.broadcasted_iota(jnp.int32, (tq, win), 1)

    def block(i, carry):
        q0 = pl.multiple_of(i * tq, tq)
        start = pl.multiple_of(jnp.clip(q0 - B_RADIUS, 0, m_len - win), B_RADIUS)
        valid = jnp.abs(row - col + (q0 - start)) <= B_RADIUS
        for p in range(N_SLAB):
            sl = slice(p * LANES, (p + 1) * LANES)
            o, lse = _band_slab(qr_ref[pl.ds(q0, tq), sl], kr_ref[pl.ds(start, win), sl], v_ref[pl.ds(start, win), sl], valid)
            o_ref[pl.ds(q0, tq), sl] = o.astype(o_ref.dtype)
            lse_ref[pl.ds(q0, tq), sl] = lse
        return carry

    lax.fori_loop(0, m_len // tq, block, 0, unroll=4)


def _attn_b1(nat, cos, sin):
    b, s, _ = nat.shape
    tq = 128
    win = min(tq + 2 * B_RADIUS, s)
    c0 = A_Q_W // B_W
    o, lse = pl.pallas_call(
        functools.partial(_attn_b1_kernel, m_len=s, tq=tq, win=win),
        grid=(b,),
        in_specs=[
            pl.BlockSpec((None, s, B_W), lambda bi: (bi, 0, c0)),
            pl.BlockSpec((None, s, B_W), lambda bi: (bi, 0, c0 + 1)),
            pl.BlockSpec((None, s, B_W), lambda bi: (bi, 0, c0 + 2)),
            _const_spec((s, LANES)),
            _const_spec((s, LANES)),
        ],
        out_specs=[pl.BlockSpec((None, s, B_W), lambda bi: (bi, 0, 0))] * 2,
        out_shape=[jax.ShapeDtypeStruct((b, s, B_W), _BF), jax.ShapeDtypeStruct((b, s, B_W), _F32)],
        scratch_shapes=[pltpu.VMEM((s, B_W), _BF), pltpu.VMEM((s, B_W), _BF)],
        compiler_params=_cparams("parallel"),
        name="attn_b1",
    )(nat, nat, nat, cos, sin)
    return o.reshape(b * s, B_W), lse.reshape(b * s, B_W)


B4_SEG = 4
B4_TQS = 32


def _b4_rope(q_ref, k_ref, c_ref, s_ref, qr_ref, kr_ref):
    half = ROPE_DIMS // 2
    for j in range(B4_SEG):
        for p in range(N_SLAB):
            sl = slice(p * LANES, (p + 1) * LANES)
            kr_ref[j, :, sl] = _rope_slab(k_ref[j, :, sl].astype(_F32), c_ref[j], s_ref[j], half).astype(_BF)
            qr_ref[j, :, sl] = (_rope_slab(q_ref[j, :, sl].astype(_F32), c_ref[j], s_ref[j], half) * Q_SCALE_LOG2).astype(_BF)


def _b4_blocks(blocks, q_ref, kr_ref, v_ref, o_ref, lse_ref):
    seg, tqs = B4_SEG, B4_TQS
    m16 = q_ref.shape[1]
    wks = min(tqs + 2 * (B_RADIUS // seg), m16)
    tq, win = seg * tqs, seg * wks
    row = lax.broadcasted_iota(jnp.int32, (tq, win), 0)
    col = lax.broadcasted_iota(jnp.int32, (tq, win), 1)
    for i in blocks:
        a_q = i * tqs
        a_k = min(max(a_q - B_RADIUS // seg, 0), m16 - wks)
        qpos = seg * (a_q + (row & (tqs - 1))) + (row >> (tqs.bit_length() - 1))
        kpos = seg * (a_k + (col & (wks - 1))) + (col >> (wks.bit_length() - 1))
        valid = jnp.abs(qpos - kpos) <= B_RADIUS
        for p in range(N_SLAB):
            sl = slice(p * LANES, (p + 1) * LANES)
            q = jnp.concatenate([q_ref[j, a_q:a_q + tqs, sl] for j in range(seg)], axis=0)
            kw = jnp.concatenate([kr_ref[j, a_k:a_k + wks, sl] for j in range(seg)], axis=0)
            vw = jnp.concatenate([v_ref[j, a_k:a_k + wks, sl] for j in range(seg)], axis=0)
            o, lse = _band_slab(q, kw, vw, valid)
            for j in range(seg):
                o_ref[j, a_q:a_q + tqs, sl] = o[j * tqs:(j + 1) * tqs].astype(o_ref.dtype)
                lse_ref[j, a_q:a_q + tqs, sl] = lse[j * tqs:(j + 1) * tqs]


def _attn_b4_kernel(q_ref, k_ref, v_ref, c_ref, s_ref, o_ref, lse_ref, qr_ref, kr_ref):
    _b4_rope(q_ref, k_ref, c_ref, s_ref, qr_ref, kr_ref)
    _b4_blocks(range(q_ref.shape[1] // B4_TQS), qr_ref, kr_ref, v_ref, o_ref, lse_ref)


def _attn_b4(perm, cos_p, sin_p):
    b, _, m16, cols = perm.shape
    pv = perm.reshape(b, B4_SEG, 4, m16, cols)
    cv = cos_p.reshape(B4_SEG, 4, m16, LANES)
    sv = sin_p.reshape(B4_SEG, 4, m16, LANES)
    blk = lambda cb: pl.BlockSpec((None, B4_SEG, None, m16, B_W), lambda bi, r: (bi, 0, r, 0, cb))
    tab = pl.BlockSpec((B4_SEG, None, m16, LANES), lambda bi, r: (0, r, 0, 0))
    o, lse = pl.pallas_call(
        _attn_b4_kernel,
        grid=(b, 4),
        in_specs=[blk(0), blk(1), blk(2), tab, tab],
        out_specs=[blk(0), blk(0)],
        out_shape=[jax.ShapeDtypeStruct((b, B4_SEG, 4, m16, B_W), _BF), jax.ShapeDtypeStruct((b, B4_SEG, 4, m16, B_W), _F32)],
        scratch_shapes=[pltpu.VMEM((B4_SEG, m16, B_W), _BF), pltpu.VMEM((B4_SEG, m16, B_W), _BF)],
        compiler_params=_cparams("parallel", "parallel"),
        name="attn_b4",
    )(pv, pv, pv, cv, sv)
    return o.reshape(b, PERM, m16, B_W), lse.reshape(b, PERM, m16, B_W)


def _b16_residues(residues, q_ref, k_ref, v_ref, c_ref, s_ref, o_ref, lse_ref):
    half = ROPE_DIMS // 2
    m16 = q_ref.shape[1]
    tq = min(128, m16)
    win = min(tq + 2 * B_RADIUS, m16)
    row = lax.broadcasted_iota(jnp.int32, (tq, win), 0)
    col = lax.broadcasted_iota(jnp.int32, (tq, win), 1)
    for r in residues:
        for i in range(m16 // tq):
            q0 = i * tq
            start = min(max(q0 - B_RADIUS, 0), m16 - win)
            valid = jnp.abs(row - col + (q0 - start)) <= B_RADIUS
            for p in range(N_SLAB):
                sl = slice(p * LANES, (p + 1) * LANES)
                q = _rope_slab(q_ref[r, q0:q0 + tq, sl].astype(_F32), c_ref[r, q0:q0 + tq, :], s_ref[r, q0:q0 + tq, :], half)
                kw = _rope_slab(k_ref[r, start:start + win, sl].astype(_F32), c_ref[r, start:start + win, :],
                                s_ref[r, start:start + win, :], half).astype(_BF)
                o, lse = _band_slab((q * Q_SCALE_LOG2).astype(_BF), kw, v_ref[r, start:start + win, sl], valid)
                o_ref[r, q0:q0 + tq, sl] = o.astype(o_ref.dtype)
                lse_ref[r, q0:q0 + tq, sl] = lse


def _even_out_kernel(x_ref, a_ref, o0_ref, l0_ref, o1_ref, l1_ref, o2_ref, l2_ref, w_ref, y_ref):
    l1, l2 = l1_ref[...], l2_ref[...]
    m12 = jnp.maximum(l1, l2)
    e1, e2 = jnp.exp(l1 - m12), jnp.exp(l2 - m12)
    den12 = e1 + e2
    o12 = _from_residue_major((e1 * o1_ref[...].astype(_F32) + e2 * o2_ref[...].astype(_F32)) / den12)
    l12 = _from_residue_major(m12 + jnp.log(den12))
    l0 = l0_ref[...]
    mx = jnp.maximum(l0, l12)
    e0, e12 = jnp.exp(l0 - mx), jnp.exp(l12 - mx)
    bo = (e0 * o0_ref[...].astype(_F32) + e12 * o12) / (e0 + e12)
    y = jnp.dot(a_ref[...], w_ref[:A_Q_W, :], preferred_element_type=_F32)
    y = y + jnp.dot(bo.astype(_BF), w_ref[A_Q_W:, :], preferred_element_type=_F32)
    y_ref[...] = x_ref[...] + y


def _even_out(x, a_o, o0, l0, o1, l1, o2, l2, w, *, s, tm):
    n, d = x.shape
    tpb = s // tm
    row = lambda width: pl.BlockSpec((tm, width), lambda i: (i, 0))
    prm = pl.BlockSpec((None, PERM, tm // PERM, B_W), lambda i: (i // tpb, 0, i % tpb, 0))
    return pl.pallas_call(
        _even_out_kernel,
        grid=(n // tm,),
        in_specs=[row(d), row(A_Q_W), row(B_W), row(B_W), prm, prm, prm, prm, _const_spec(w.shape)],
        out_specs=row(d),
        out_shape=jax.ShapeDtypeStruct((n, d), _F32),
        compiler_params=_cparams("parallel"),
        name="even_out",
    )(x, a_o, o0, l0, o1, l1, o2, l2, w)


def _ffn_kernel(*refs, chunk, final_norm, mixer_proj):
    if mixer_proj:
        x_ref, hs_ref, gate_ref, wo_ref, g_ref, wg_ref, wu_ref, wd_ref, gf_ref, y_ref, h_ref, act_ref = refs
        rows = h_ref.shape[0]
        hs = jnp.swapaxes(hs_ref[...].astype(_F32), 0, 1).reshape(rows, D_RNN)
        a = (hs * gate_ref[...].reshape(rows, D_RNN).astype(_F32)).astype(_BF)
        x = x_ref[...].reshape(rows, D_MODEL) + jnp.dot(a, wo_ref[...], preferred_element_type=_F32)
    else:
        x_ref, g_ref, wg_ref, wu_ref, wd_ref, gf_ref, y_ref, h_ref, act_ref = refs
        x = x_ref[...]
    h_ref[...] = _rms_rows(x, g_ref[...]).astype(_BF)
    for c in range(D_FF // chunk):
        sl = slice(c * chunk, (c + 1) * chunk)
        gate = jnp.dot(h_ref[...], wg_ref[:, sl], preferred_element_type=_F32)
        up = jnp.dot(h_ref[...], wu_ref[:, sl], preferred_element_type=_F32)
        act_ref[:, sl] = (gate * _sigmoid(gate) * up).astype(_BF)
    y = x + jnp.dot(act_ref[...], wd_ref[...], preferred_element_type=_F32)
    if final_norm:
        y = _rms_rows(y, gf_ref[...])
    y_ref[...] = y.reshape(y_ref.shape)


def _ffn(x, g, w_gate_up, w_down, g_final, *, tm, final_norm, hsum_tm=None, gate=None, w_mixer=None):
    b, s, d = x.shape
    mixer_proj = hsum_tm is not None
    const = lambda shape, *idx: pl.BlockSpec(shape, lambda *_: idx, pipeline_mode=pl.Buffered(1))
    weights = [const((1, d), 0, 0), const((d, D_FF), 0, 0), const((d, D_FF), 0, 1), const((D_FF, d), 0, 0), const((1, d), 0, 0)]
    w_args = [g, w_gate_up, w_gate_up, w_down, g_final]
    if mixer_proj:
        tt = tm // SCAN_BATCH
        grid = (b // SCAN_BATCH, s // tt)
        xs = pl.BlockSpec((SCAN_BATCH, tt, d), lambda gi, c: (gi, c, 0))
        in_specs = [xs, pl.BlockSpec((None, tt, SCAN_BATCH, d), lambda gi, c: (gi, c, 0, 0)), xs, const(w_mixer.shape, 0, 0)] + weights
        args = [x, hsum_tm, gate, w_mixer] + w_args
        sem = ("parallel", "parallel")
    else:
        x = x.reshape(b * s, d)
        grid = (b * s // tm,)
        xs = pl.BlockSpec((tm, d), lambda i: (i, 0))
        in_specs = [xs] + weights
        args = [x] + w_args
        sem = ("parallel",)
    y = pl.pallas_call(
        functools.partial(_ffn_kernel, chunk=256, final_norm=final_norm, mixer_proj=mixer_proj),
        grid=grid,
        in_specs=in_specs,
        out_specs=xs,
        out_shape=jax.ShapeDtypeStruct(x.shape, _F32),
        scratch_shapes=[pltpu.VMEM((tm, d), _BF), pltpu.VMEM((tm, D_FF), _BF)],
        compiler_params=_cparams(*sem),
        name="ffn_mix" if mixer_proj else "ffn",
    )(*args)
    return y.reshape(b, s, d)


SCAN_BATCH = 8
SCAN_UNROLL = 4
HALO = 8


def _conv_kernel(u_ref, prev_ref, next_ref, w_ref, b_ref, o_ref, *, tc):
    c = pl.program_id(1)
    has_prev = (c > 0).astype(_F32)
    has_next = (c < pl.num_programs(1) - 1).astype(_F32)
    t = lax.broadcasted_iota(jnp.int32, (tc, LANES), 0)
    for cb in range(D_RNN // LANES):
        sl = slice(cb * LANES, (cb + 1) * LANES)
        w = w_ref[:, sl]
        bias = b_ref[:, sl]
        rows = []
        for bi in range(SCAN_BATCH):
            u = u_ref[bi, :, sl]
            p_last = prev_ref[bi, HALO - 1:HALO, sl] * has_prev
            n0 = next_ref[bi, 0:1, sl] * has_next
            n1 = next_ref[bi, 1:2, sl] * has_next
            um1 = jnp.where(t == 0, p_last, pltpu.roll(u, 1, 0))
            up1 = jnp.where(t == tc - 1, n0, pltpu.roll(u, tc - 1, 0))
            up2 = jnp.where(t == tc - 2, n0, jnp.where(t == tc - 1, n1, pltpu.roll(u, tc - 2, 0)))
            rows.append(um1 * w[0:1] + u * w[1:2] + up1 * w[2:3] + up2 * w[3:4] + bias)
        o_ref[:, :, sl] = jnp.swapaxes(jnp.stack(rows, axis=0), 0, 1)


def _conv(u, w, bias, *, tc):
    b, s, d = u.shape
    last = s // HALO - 1
    per = tc // HALO
    return pl.pallas_call(
        functools.partial(_conv_kernel, tc=tc),
        grid=(b // SCAN_BATCH, s // tc),
        in_specs=[
            pl.BlockSpec((SCAN_BATCH, tc, d), lambda g, c: (g, c, 0)),
            pl.BlockSpec((SCAN_BATCH, HALO, d), lambda g, c: (g, jnp.maximum(c * per - 1, 0), 0)),
            pl.BlockSpec((SCAN_BATCH, HALO, d), lambda g, c: (g, jnp.minimum((c + 1) * per, last), 0)),
            _const_spec(w.shape),
            _const_spec((1, d)),
        ],
        out_specs=pl.BlockSpec((None, tc, SCAN_BATCH, d), lambda g, c: (g, c, 0, 0)),
        out_shape=jax.ShapeDtypeStruct((b // SCAN_BATCH, s, SCAN_BATCH, d), _F32),
        compiler_params=_cparams("parallel", "parallel"),
        name="conv",
    )(u, u, u, w, bias)


def _scan_kernel(*refs, tc, reverse, finalize):
    if finalize:
        u_ref, w_ref, ba_ref, bx_ref, lam_ref, hf_ref, o_ref, a_s, b_s, h_s, carry = refs
    else:
        u_ref, w_ref, ba_ref, bx_ref, lam_ref, o_ref, a_s, b_s, h_s, carry = refs
    rows = SCAN_BATCH * tc

    @pl.when(pl.program_id(1) == 0)
    def _():
        carry[...] = jnp.zeros_like(carry)

    for n in range(LRU_BLOCKS):
        sl = slice(n * LRU_BW, (n + 1) * LRU_BW)
        lam = lam_ref[:, sl]
        c2 = (-4.0 * LOG2E) * (jnp.maximum(-lam, 0.0) + jnp.log1p(jnp.exp(-jnp.abs(lam))))
        xh = u_ref[:, :, sl].reshape(rows, LRU_BW)
        z = jnp.dot(xh.astype(_BF), w_ref[n], preferred_element_type=_F32)
        th_a = jnp.tanh(z[:, :LRU_BW] + ba_ref[:, sl])
        th_x = jnp.tanh(z[:, LRU_BW:] + bx_ref[:, sl])
        a = jnp.exp2(c2 + c2 * th_a)
        a_s[n] = a
        v = 1.0 - a * a
        root = jnp.where(v > 0.0, v * lax.rsqrt(v), 0.0)
        b_s[n] = root * ((1.0 + th_x) * xh)

    def step(i, hs):
        t = (tc - 1 - i) if reverse else i
        idx = pl.ds(pl.multiple_of(t * SCAN_BATCH, SCAN_BATCH), SCAN_BATCH)
        new = []
        for n in range(LRU_BLOCKS):
            h = a_s[n, idx, :] * hs[n] + b_s[n, idx, :]
            h_s[n, idx, :] = h
            new.append(h)
        return tuple(new)

    hs = lax.fori_loop(0, tc, step, tuple(carry[n] for n in range(LRU_BLOCKS)), unroll=SCAN_UNROLL)
    for n in range(LRU_BLOCKS):
        carry[n] = hs[n]

    for n in range(LRU_BLOCKS):
        sl = slice(n * LRU_BW, (n + 1) * LRU_BW)
        h = h_s[n].reshape(tc, SCAN_BATCH, LRU_BW)
        if finalize:
            o_ref[:, :, sl] = (hf_ref[:, :, sl] + h).astype(o_ref.dtype)
        else:
            o_ref[:, :, sl] = h


def _scan(u, w, ba, bx, lam, *, tc, reverse, h_fwd=None):
    g8, s, _, d = u.shape
    nt = s // tc
    finalize = h_fwd is not None
    tmap = (lambda g, c: (g, nt - 1 - c, 0, 0)) if reverse else (lambda g, c: (g, c, 0, 0))
    blk = pl.BlockSpec((None, tc, SCAN_BATCH, d), tmap)
    in_specs = [blk, _const_spec(w.shape), _const_spec((1, d)), _const_spec((1, d)), _const_spec((1, d))]
    args = [u, w, ba, bx, lam]
    if finalize:
        in_specs += [blk]
        args += [h_fwd]
    work = pltpu.VMEM((LRU_BLOCKS, SCAN_BATCH * tc, LRU_BW), _F32)
    return pl.pallas_call(
        functools.partial(_scan_kernel, tc=tc, reverse=reverse, finalize=finalize),
        grid=(g8, nt),
        in_specs=in_specs,
        out_specs=blk,
        out_shape=jax.ShapeDtypeStruct(u.shape, _BF if finalize else _F32),
        scratch_shapes=[work, work, work, pltpu.VMEM((LRU_BLOCKS, SCAN_BATCH, LRU_BW), _F32)],
        compiler_params=_cparams("parallel", "arbitrary"),
        name="scan_bwd" if reverse else "scan_fwd",
    )(*args)


def _rope_tables(pos_lo, pos_hi, n_rot, theta, s):
    half = n_rot // 2
    inv = jnp.power(jnp.float32(theta), -jnp.arange(half, dtype=_F32) * (2.0 / n_rot))

    def part(pos):
        ang = pos.astype(_F32)[:, None] * inv[None, :]
        c, sn = jnp.cos(ang), jnp.sin(ang)
        return jnp.concatenate([c, c], axis=-1), jnp.concatenate([-sn, sn], axis=-1)

    cs, sns = [], []
    for pos in (pos_lo, pos_hi):
        if pos is not None:
            c, sn = part(pos)
            cs.append(c)
            sns.append(sn)
    width = sum(c.shape[-1] for c in cs)
    cs.append(jnp.ones((s, HEAD_DIM - width), _F32))
    sns.append(jnp.zeros((s, HEAD_DIM - width), _F32))
    c = jnp.concatenate(cs, axis=-1)
    sn = jnp.concatenate(sns, axis=-1)
    return jnp.concatenate([c, c], axis=-1), jnp.concatenate([sn, sn], axis=-1)


def _even_w_in_layout(w):
    d = w.shape[0]
    aq = w[:, :A_Q_W].reshape(d, A_Q_HEADS, HEAD_DIM)[:, jnp.array(A_HEAD_ORDER), :].reshape(d, A_Q_W)
    akv = w[:, A_Q_W:A_Q_W + 2 * LANES]
    rest = w[:, A_Q_W + 2 * LANES:]
    w_nat = jnp.concatenate([aq, rest[:, :3 * B_W], akv], axis=1).astype(_BF)
    return w_nat, rest[:, 3 * B_W:].astype(_BF)


def _even_w_out_layout(w):
    d = w.shape[1]
    wa = w[:A_Q_W].reshape(A_Q_HEADS, HEAD_DIM, d)[jnp.array(A_HEAD_ORDER)].reshape(A_Q_W, d)
    return jnp.concatenate([wa, w[A_Q_W:]], axis=0).astype(_BF)


def kernel(x, mix_norm, ffn_norm, final_norm, even_w_in, even_q_norm, even_k_norm, even_w_out, odd_w_in, odd_conv_w, odd_conv_b, odd_gate_a_w, odd_gate_a_b, odd_gate_x_w, odd_gate_x_b, odd_lambda, odd_w_out, ffn_w_gate_up, ffn_w_down):
    b, s, d = x.shape
    n = b * s
    depth = mix_norm.shape[0]
    m16 = s // PERM
    assert d == D_MODEL and s % 512 == 0 and b % SCAN_BATCH == 0

    pos = jnp.arange(s, dtype=jnp.int32)
    cos_a, sin_a = _rope_tables(pos // GRID_W, pos % GRID_W, HEAD_DIM // 2, AXIAL_THETA, s)
    cos_b, sin_b = _rope_tables(pos, None, ROPE_DIMS, ROPE_THETA, s)
    cos_p = cos_b.reshape(m16, PERM, LANES).transpose(1, 0, 2)
    sin_p = sin_b.reshape(m16, PERM, LANES).transpose(1, 0, 2)

    tm = 512
    xf = x.reshape(n, d)
    for layer in range(depth):
        j = layer // 2
        g_mix = mix_norm[layer].reshape(1, d)
        if layer % 2 == 0:
            w_nat, w_perm = _even_w_in_layout(even_w_in[j])
            nat, perm = _even_in(xf, g_mix, w_nat, w_perm, b=b, s=s, tm=tm)
            nat3 = nat.reshape(b, s, NAT_COLS)
            gq = jnp.tile(even_q_norm[j], 2).reshape(1, LANES)
            gk = jnp.tile(even_k_norm[j], 2).reshape(1, LANES)
            a_o, o2, l2 = _attn_a(nat3, perm, cos_a, sin_a, cos_p, sin_p, gq, gk)
            o0, l0 = _attn_b1(nat3, cos_b, sin_b)
            o1, l1 = _attn_b4(perm, cos_p, sin_p)
            xf = _even_out(xf, a_o.reshape(n, A_Q_W), o0, l0, o1, l1, o2, l2, _even_w_out_layout(even_w_out[j]), s=s, tm=tm)
            mix = {}
        else:
            gate, u = _odd_in(xf, g_mix, odd_w_in[j].astype(_BF), tm=tm)
            uc = _conv(u.reshape(b, s, D_RNN), 0.5 * odd_conv_w[j], 0.5 * odd_conv_b[j].reshape(1, D_RNN), tc=128)
            h = None
            for direction in range(2):
                w = jnp.concatenate([odd_gate_a_w[j, direction], odd_gate_x_w[j, direction]], axis=-1).astype(_BF)
                h = _scan(uc, w, 0.5 * odd_gate_a_b[j, direction].reshape(1, D_RNN), 0.5 * odd_gate_x_b[j, direction].reshape(1, D_RNN),
                          odd_lambda[j, direction].reshape(1, D_RNN), tc=128, reverse=direction == 1, h_fwd=h)
            mix = dict(hsum_tm=h, gate=gate.reshape(b, s, D_RNN), w_mixer=odd_w_out[j].astype(_BF))
        xf = _ffn(xf.reshape(b, s, d), ffn_norm[layer].reshape(1, d), ffn_w_gate_up[layer].astype(_BF), ffn_w_down[layer].astype(_BF),
                  final_norm.reshape(1, d), tm=tm, final_norm=layer == depth - 1, **mix).reshape(n, d)
    return xf.reshape(b, s, d)
```
